```python
import jax, jax.numpy as jnp
from jax import lax
import numpy as np

D_MODEL = 1024
BATCH = 1
SEQ = 16384
DEPTH = 2
DEC_BATCH = 32
DEC_SEQ = 8
PAST_LEN = 16384
PAGE_SIZE = 128

HEAD_DIM = 64
SB_HEADS = 4
NSA_HEADS = 8
NSA_KV_GROUPS = 2
NSA_GROUP_SIZE = NSA_HEADS // NSA_KV_GROUPS
FOX_HEADS = 4
CMP_BLOCK = 64
SEL_BLOCK = 64
SEL_TOPK = 16
WINDOW = 512
D_FF = 2816
Q_BLOCK = 128
N_BRANCHES = 3
RMS_EPS = 1e-6
NEG_INF = -1e30
FORCE_SCORE = 1e4

SB_W = SB_HEADS * HEAD_DIM
NSA_W = NSA_HEADS * HEAD_DIM
NSA_KV_W = NSA_KV_GROUPS * HEAD_DIM
FOX_W = FOX_HEADS * HEAD_DIM
IN_SPLITS = (SB_W, SB_W, SB_W, NSA_W, NSA_KV_W, NSA_KV_W, NSA_KV_W, NSA_KV_W, NSA_KV_W, NSA_KV_W,
             3 * NSA_HEADS, FOX_W, FOX_W, FOX_W, FOX_HEADS, N_BRANCHES * D_MODEL)
IN_COLS = 3 * SB_W + NSA_W + 6 * NSA_KV_W + 3 * NSA_HEADS + 3 * FOX_W + FOX_HEADS + N_BRANCHES * D_MODEL

kernel_name = "hybrid_sb_nsa_fox_macaron_step"


def rms_norm(x, g):
    xf = x.astype(jnp.float32)
    y = xf * lax.rsqrt(jnp.mean(xf * xf, axis=-1, keepdims=True) + RMS_EPS)
    return (y * g.astype(jnp.float32)).astype(x.dtype)


def half_ffn(x, pre_g, post_g, w_gu, w_down):
    h = rms_norm(x, pre_g) @ w_gu
    gate, up = jnp.split(h, 2, axis=-1)
    return x + 0.5 * rms_norm((jax.nn.silu(gate) * up) @ w_down, post_g)


def masked_softmax(logits, mask):
    logits = jnp.where(mask, logits, NEG_INF)
    m = jnp.max(logits, axis=-1, keepdims=True)
    p = jnp.where(mask, jnp.exp(logits - m), 0.0)
    return p / jnp.maximum(jnp.sum(p, axis=-1, keepdims=True), 1e-30)


def alibi_slopes():
    return jnp.exp2(-8.0 * jnp.arange(1, NSA_HEADS + 1, dtype=jnp.float32) / NSA_HEADS)


def project_inputs(h, w_in, b_forget):
    B, T, _ = h.shape
    y = h @ w_in
    parts = []
    start = 0
    for size in IN_SPLITS:
        parts.append(y[..., start:start + size])
        start += size
    (sb_q, sb_k, sb_v, nsa_q, cmp_k, cmp_v, sel_k, sel_v, win_k, win_v,
     nsa_gate, fox_q, fox_k, fox_v, fox_f, br_gate) = parts

    def heads(a, n):
        return a.reshape(B, T, n, HEAD_DIM)

    def kv(k, v, n):
        return jnp.stack([heads(k, n), heads(v, n)], axis=2)

    return dict(
        sb_q=heads(sb_q, SB_HEADS), sb_kv=kv(sb_k, sb_v, SB_HEADS),
        nsa_q=heads(nsa_q, NSA_HEADS),
        cmp_kv=kv(cmp_k, cmp_v, NSA_KV_GROUPS),
        sel_kv=kv(sel_k, sel_v, NSA_KV_GROUPS),
        win_kv=kv(win_k, win_v, NSA_KV_GROUPS),
        nsa_gate=jax.nn.sigmoid(nsa_gate).reshape(B, T, 3, NSA_HEADS),
        fox_q=heads(fox_q, FOX_HEADS), fox_kv=kv(fox_k, fox_v, FOX_HEADS),
        log_f=jax.nn.log_sigmoid((fox_f + b_forget).astype(jnp.float32)),
        br_gate=jax.nn.sigmoid(br_gate).reshape(B, T, N_BRANCHES, D_MODEL))


def compress_blocks(kv, cmp_w):
    B, T = kv.shape[:2]
    nb = T // CMP_BLOCK
    blocks = kv[:, :nb * CMP_BLOCK].reshape(B, nb, CMP_BLOCK, 2, NSA_KV_GROUPS, HEAD_DIM)
    ck = jnp.einsum('bnlcgd,l->bncgd', blocks, cmp_w)
    end = jnp.arange(nb, dtype=jnp.int32) * CMP_BLOCK + (CMP_BLOCK - 1)
    return ck[:, :, 0], ck[:, :, 1], end


def selection_blocks(kv):
    B, T = kv.shape[:2]
    nbs = -(-T // SEL_BLOCK)
    kv = jnp.pad(kv, ((0, 0), (0, nbs * SEL_BLOCK - T), (0, 0), (0, 0), (0, 0)))
    blocks = kv.reshape(B, nbs, SEL_BLOCK, 2, NSA_KV_GROUPS, HEAD_DIM).transpose(0, 4, 1, 2, 3, 5)
    return blocks[..., 0, :], blocks[..., 1, :]


def stick_breaking_attn(q, k, v, q_pos, k_pos):
    z = jnp.einsum('bthd,bshd->bhts', q, k, preferred_element_type=jnp.float32) * HEAD_DIM ** -0.5
    mask = k_pos[None, :] < q_pos[:, None]
    log_keep = jnp.where(mask, jax.nn.log_sigmoid(-z), 0.0)
    after = lax.cumsum(log_keep, axis=3, reverse=True) - log_keep
    w = jnp.where(mask, jnp.exp(jax.nn.log_sigmoid(z) + after), 0.0)
    return jnp.einsum('bhts,bshd->bthd', w.astype(v.dtype), v)


def forgetting_attn(q, k, v, fq, fk, q_pos, k_pos):
    z = jnp.einsum('bthd,bshd->bhts', q, k, preferred_element_type=jnp.float32) * HEAD_DIM ** -0.5
    logits = z + jnp.swapaxes(fq, 1, 2)[..., None] - jnp.swapaxes(fk, 1, 2)[:, :, None, :]
    p = masked_softmax(logits, k_pos[None, :] <= q_pos[:, None])
    return jnp.einsum('bhts,bshd->bthd', p.astype(v.dtype), v)


def nsa_attn(q, gates, q_pos, cmp_k, cmp_v, cmp_end, sel_k, sel_v, win_k, win_v, win_pos):
    B, Tq = q.shape[:2]
    scale = HEAD_DIM ** -0.5
    qg = q.reshape(B, Tq, NSA_KV_GROUPS, NSA_GROUP_SIZE, HEAD_DIM)
    slopes = alibi_slopes().reshape(NSA_KV_GROUPS, NSA_GROUP_SIZE)[None, :, :, None, None]
    dist_c = (q_pos[:, None] - cmp_end[None, :]).astype(jnp.float32)
    lc = jnp.einsum('btgrd,bngd->bgrtn', qg, cmp_k, preferred_element_type=jnp.float32) * scale - slopes * dist_c
    pc = masked_softmax(lc, cmp_end[None, :] <= q_pos[:, None])
    o_cmp = jnp.einsum('bgrtn,bngd->btgrd', pc.astype(cmp_v.dtype), cmp_v)
    nbs = sel_k.shape[2]
    score = jnp.sum(pc, axis=2)
    score = jnp.pad(score, ((0, 0), (0, 0), (0, 0), (0, nbs - score.shape[-1])))
    blk = jnp.arange(nbs, dtype=jnp.int32)[None, :]
    cur = (q_pos // SEL_BLOCK)[:, None]
    forced = (blk == cur) | (blk == cur - 1) | (blk == 0)
    score = jnp.where(forced, FORCE_SCORE, jnp.where(blk > cur, -FORCE_SCORE, score))
    _, idx = lax.top_k(score, min(SEL_TOPK, nbs))
    b_ix = jnp.arange(B)[:, None, None, None]
    g_ix = jnp.arange(NSA_KV_GROUPS)[None, :, None, None]
    n_sel = idx.shape[-1] * SEL_BLOCK
    ks = sel_k[b_ix, g_ix, idx].reshape(B, NSA_KV_GROUPS, Tq, n_sel, HEAD_DIM)
    vs = sel_v[b_ix, g_ix, idx].reshape(B, NSA_KV_GROUPS, Tq, n_sel, HEAD_DIM)
    s_pos = (idx[..., None] * SEL_BLOCK + jnp.arange(SEL_BLOCK, dtype=jnp.int32)).reshape(B, NSA_KV_GROUPS, Tq, n_sel)
    dist_s = (q_pos[None, None, :, None] - s_pos).astype(jnp.float32)[:, :, None]
    ls = jnp.einsum('btgrd,bgtsd->bgrts', qg, ks, preferred_element_type=jnp.float32) * scale - slopes * dist_s
    ps = masked_softmax(ls, (s_pos <= q_pos[None, None, :, None])[:, :, None])
    o_sel = jnp.einsum('bgrts,bgtsd->btgrd', ps.astype(vs.dtype), vs)
    dist_w = q_pos[:, None] - win_pos[None, :]
    mask_w = (dist_w >= 0) & (dist_w < WINDOW) & (win_pos[None, :] >= 0)
    lw = jnp.einsum('btgrd,bsgd->bgrts', qg, win_k, preferred_element_type=jnp.float32) * scale - slopes * dist_w.astype(jnp.float32)
    pw = masked_softmax(lw, mask_w)
    o_win = jnp.einsum('bgrts,bsgd->btgrd', pw.astype(win_v.dtype), win_v)
    g = gates.reshape(B, Tq, 3, NSA_KV_GROUPS, NSA_GROUP_SIZE)[..., None]
    o = g[:, :, 0] * o_cmp + g[:, :, 1] * o_sel + g[:, :, 2] * o_win
    return o.reshape(B, Tq, NSA_HEADS, HEAD_DIM)


def attend_block(q_pos, sb_q, nsa_q, nsa_gate, fox_q, fox_fq, k_pos, sb_k, sb_v, fox_k, fox_v, fox_fk,
                 cmp_k, cmp_v, cmp_end, sel_k, sel_v, win_k, win_v, win_pos):
    o_sb = stick_breaking_attn(sb_q, sb_k, sb_v, q_pos, k_pos)
    o_nsa = nsa_attn(nsa_q, nsa_gate, q_pos, cmp_k, cmp_v, cmp_end, sel_k, sel_v, win_k, win_v, win_pos)
    o_fox = forgetting_attn(fox_q, fox_k, fox_v, fox_fq, fox_fk, q_pos, k_pos)
    return o_sb, o_nsa, o_fox


def prompt_mixer(h, w_in, b_forget, cmp_w):
    B, T, _ = h.shape
    p = project_inputs(h, w_in, b_forget)
    k_pos = jnp.arange(T, dtype=jnp.int32)
    fox_F = lax.cumsum(p['log_f'], axis=1)
    cmp_k, cmp_v, cmp_end = compress_blocks(p['cmp_kv'], cmp_w)
    sel_k, sel_v = selection_blocks(p['sel_kv'])
    win_pad = jnp.pad(p['win_kv'], ((0, 0), (WINDOW, 0), (0, 0), (0, 0), (0, 0)))
    sb_k, sb_v = p['sb_kv'][:, :, 0], p['sb_kv'][:, :, 1]
    fox_k, fox_v = p['fox_kv'][:, :, 0], p['fox_kv'][:, :, 1]

    def one_block(i):
        s = i * Q_BLOCK
        sl = lambda a: lax.dynamic_slice_in_dim(a, s, Q_BLOCK, axis=1)
        q_pos = s + jnp.arange(Q_BLOCK, dtype=jnp.int32)
        win = lax.dynamic_slice_in_dim(win_pad, s, WINDOW + Q_BLOCK, axis=1)
        win_pos = s - WINDOW + jnp.arange(WINDOW + Q_BLOCK, dtype=jnp.int32)
        return attend_block(q_pos, sl(p['sb_q']), sl(p['nsa_q']), sl(p['nsa_gate']), sl(p['fox_q']), sl(fox_F),
                            k_pos, sb_k, sb_v, fox_k, fox_v, fox_F, cmp_k, cmp_v, cmp_end, sel_k, sel_v,
                            win[:, :, 0], win[:, :, 1], win_pos)

    o_sb, o_nsa, o_fox = lax.map(one_block, jnp.arange(T // Q_BLOCK, dtype=jnp.int32))
    unblock = lambda o: jnp.swapaxes(o, 0, 1).reshape(B, T, o.shape[3], o.shape[4])
    return p, unblock(o_sb), unblock(o_nsa), unblock(o_fox)


def sample_mixer(h, past_sb, past_cmp, past_sel, past_fox, past_logf, win_buf, w_in, b_forget, cmp_w):
    B, T, _ = h.shape
    p = project_inputs(h, w_in, b_forget)
    past_len = past_sb.shape[1]
    cat = lambda past, new: jnp.concatenate([past, new], axis=1)
    sb_all = cat(past_sb, p['sb_kv'])
    fox_all = cat(past_fox, p['fox_kv'])
    fox_F = lax.cumsum(cat(past_logf, p['log_f']).astype(jnp.float32), axis=1)
    k_pos = jnp.arange(past_len + T, dtype=jnp.int32)
    q_pos = past_len + jnp.arange(T, dtype=jnp.int32)
    cmp_k, cmp_v, cmp_end = compress_blocks(cat(past_cmp, p['cmp_kv']), cmp_w)
    sel_k, sel_v = selection_blocks(cat(past_sel, p['sel_kv']))
    wb = win_buf.shape[1]
    win_all = cat(win_buf, p['win_kv'])
    win_pos = past_len - wb + jnp.arange(wb + T, dtype=jnp.int32)
    o_sb, o_nsa, o_fox = attend_block(q_pos, p['sb_q'], p['nsa_q'], p['nsa_gate'], p['fox_q'], fox_F[:, past_len:],
                                      k_pos, sb_all[:, :, 0], sb_all[:, :, 1], fox_all[:, :, 0], fox_all[:, :, 1], fox_F,
                                      cmp_k, cmp_v, cmp_end, sel_k, sel_v, win_all[:, :, 0], win_all[:, :, 1], win_pos)
    return p, o_sb, o_nsa, o_fox, win_all[:, T:]


def merge_branches(o_sb, o_nsa, o_fox, br_gate, w_br_sb, w_br_nsa, w_br_fox, w_out):
    B, T = o_sb.shape[:2]
    y_sb = o_sb.reshape(B, T, SB_W) @ w_br_sb
    y_nsa = o_nsa.reshape(B, T, NSA_W) @ w_br_nsa
    y_fox = o_fox.reshape(B, T, FOX_W) @ w_br_fox
    mixed = br_gate[:, :, 0] * y_sb + br_gate[:, :, 1] * y_nsa + br_gate[:, :, 2] * y_fox
    return mixed @ w_out


def gather_pages(cache_l, page_table):
    g = cache_l[page_table]
    return g.reshape(g.shape[0], g.shape[1] * g.shape[2], *g.shape[3:])


def setup_inputs(seed: int = 0) -> dict:
    key = jax.random.key(seed)
    ks = iter(jax.random.split(key, 40))
    nrm = lambda shape, scale: jax.random.normal(next(ks), shape, jnp.float32) * scale
    n_pages = PAST_LEN // PAGE_SIZE
    n_pool = (DEC_BATCH * n_pages * 5) // 4
    win_buf = min(WINDOW, PAST_LEN)
    gain = lambda: 1.0 + nrm((DEPTH, D_MODEL), 0.02)
    page_table = jax.random.permutation(next(ks), n_pool)[:DEC_BATCH * n_pages].reshape(DEC_BATCH, n_pages).astype(jnp.int32)
    return {
        'x_prompt': nrm((BATCH, SEQ, D_MODEL), 1.0),
        'x_sample': nrm((DEC_BATCH, DEC_SEQ, D_MODEL), 1.0),
        'cache_sb_kv': nrm((DEPTH, n_pool, PAGE_SIZE, 2, SB_HEADS, HEAD_DIM), 1.0),
        'cache_cmp_kv': nrm((DEPTH, n_pool, PAGE_SIZE, 2, NSA_KV_GROUPS, HEAD_DIM), 1.0),
        'cache_sel_kv': nrm((DEPTH, n_pool, PAGE_SIZE, 2, NSA_KV_GROUPS, HEAD_DIM), 1.0),
        'cache_fox_kv': nrm((DEPTH, n_pool, PAGE_SIZE, 2, FOX_HEADS, HEAD_DIM), 1.0),
        'cache_fox_logf': jax.nn.log_sigmoid(3.0 + nrm((DEPTH, n_pool, PAGE_SIZE, FOX_HEADS), 1.0)),
        'state_win_kv': nrm((DEPTH, DEC_BATCH, win_buf, 2, NSA_KV_GROUPS, HEAD_DIM), 1.0),
        'page_table': page_table,
        'ffn1_pre_g': gain(), 'ffn1_post_g': gain(),
        'ffn1_w_gu': nrm((DEPTH, D_MODEL, 2 * D_FF), D_MODEL ** -0.5),
        'ffn1_w_down': nrm((DEPTH, D_FF, D_MODEL), D_FF ** -0.5),
        'mix_pre_g': gain(), 'mix_post_g': gain(),
        'w_in': nrm((DEPTH, D_MODEL, IN_COLS), D_MODEL ** -0.5),
        'b_forget': 3.0 + nrm((DEPTH, FOX_HEADS), 0.1),
        'cmp_w': (1.0 + nrm((DEPTH, CMP_BLOCK), 0.1)) / CMP_BLOCK,
        'w_br_sb': nrm((DEPTH, SB_W, D_MODEL), SB_W ** -0.5),
        'w_br_nsa': nrm((DEPTH, NSA_W, D_MODEL), NSA_W ** -0.5),
        'w_br_fox': nrm((DEPTH, FOX_W, D_MODEL), FOX_W ** -0.5),
        'w_out': nrm((DEPTH, D_MODEL, D_MODEL), D_MODEL ** -0.5),
        'ffn2_pre_g': gain(), 'ffn2_post_g': gain(),
        'ffn2_w_gu': nrm((DEPTH, D_MODEL, 2 * D_FF), D_MODEL ** -0.5),
        'ffn2_w_down': nrm((DEPTH, D_FF, D_MODEL), D_FF ** -0.5),
    }


def reference(x_prompt, x_sample, cache_sb_kv, cache_cmp_kv, cache_sel_kv, cache_fox_kv, cache_fox_logf,
              state_win_kv, page_table, ffn1_pre_g, ffn1_post_g, ffn1_w_gu, ffn1_w_down, mix_pre_g, mix_post_g,
              w_in, b_forget, cmp_w, w_br_sb, w_br_nsa, w_br_fox, w_out, ffn2_pre_g, ffn2_post_g, ffn2_w_gu,
              ffn2_w_down):
    x_p, x_s = x_prompt, x_sample
    sb_p, sb_s, cmp_p, cmp_s, sel_p, sel_s, fox_p, fox_s, lf_p, lf_s, win_p, win_s = ([] for _ in range(12))
    for l in range(DEPTH):
        x_p = half_ffn(x_p, ffn1_pre_g[l], ffn1_post_g[l], ffn1_w_gu[l], ffn1_w_down[l])
        x_s = half_ffn(x_s, ffn1_pre_g[l], ffn1_post_g[l], ffn1_w_gu[l], ffn1_w_down[l])
        pp, o_sb, o_nsa, o_fox = prompt_mixer(rms_norm(x_p, mix_pre_g[l]), w_in[l], b_forget[l], cmp_w[l])
        x_p = x_p + rms_norm(merge_branches(o_sb, o_nsa, o_fox, pp['br_gate'], w_br_sb[l], w_br_nsa[l],
                                            w_br_fox[l], w_out[l]), mix_post_g[l])
        ps, o_sb, o_nsa, o_fox, new_win = sample_mixer(
            rms_norm(x_s, mix_pre_g[l]), gather_pages(cache_sb_kv[l], page_table),
            gather_pages(cache_cmp_kv[l], page_table), gather_pages(cache_sel_kv[l], page_table),
            gather_pages(cache_fox_kv[l], page_table), gather_pages(cache_fox_logf[l], page_table),
            state_win_kv[l], w_in[l], b_forget[l], cmp_w[l])
        x_s = x_s + rms_norm(merge_branches(o_sb, o_nsa, o_fox, ps['br_gate'], w_br_sb[l], w_br_nsa[l],
                                            w_br_fox[l], w_out[l]), mix_post_g[l])
        x_p = half_ffn(x_p, ffn2_pre_g[l], ffn2_post_g[l], ffn2_w_gu[l], ffn2_w_down[l])
        x_s = half_ffn(x_s, ffn2_pre_g[l], ffn2_post_g[l], ffn2_w_gu[l], ffn2_w_down[l])
        sb_p.append(pp['sb_kv']); sb_s.append(ps['sb_kv'])
        cmp_p.append(pp['cmp_kv']); cmp_s.append(ps['cmp_kv'])
        sel_p.append(pp['sel_kv']); sel_s.append(ps['sel_kv'])
        fox_p.append(pp['fox_kv']); fox_s.append(ps['fox_kv'])
        lf_p.append(pp['log_f']); lf_s.append(ps['log_f'])
        win_p.append(pp['win_kv'][:, -min(WINDOW, x_p.shape[1]):]); win_s.append(new_win)
    return (x_p, x_s, jnp.stack(sb_p), jnp.stack(sb_s), jnp.stack(cmp_p), jnp.stack(cmp_s),
            jnp.stack(sel_p), jnp.stack(sel_s), jnp.stack(fox_p), jnp.stack(fox_s),
            jnp.stack(lf_p), jnp.stack(lf_s), jnp.stack(win_p), jnp.stack(win_s))
```

```python
import functools

import jax
import jax.numpy as jnp
from jax import lax
from jax.experimental import pallas as pl
from jax.experimental.pallas import tpu as pltpu

F32 = jnp.float32
BF16 = jnp.bfloat16

D_MODEL = 1024
HEAD_DIM = 64
SB_HEADS = 4
NSA_HEADS = 8
NSA_KV_GROUPS = 2
NSA_GROUP_SIZE = NSA_HEADS // NSA_KV_GROUPS
FOX_HEADS = 4
CMP_BLOCK = 64
SEL_BLOCK = 64
SEL_TOPK = 16
WINDOW = 512
D_FF = 2816
PAGE_SIZE = 128
RMS_EPS = 1e-6
NEG_INF = -1e30
FORCE_SCORE = 1e4

LANES = 128
V7X_VMEM_LIMIT_BYTES = 56 * 1024 * 1024
ATT_TILE = 256
Q_TILE = 128
FFN_CHUNK = 1408
PAGES_PER_STEP = 8
SB_EXIT = 110.0

KV_ROWS_SB = 2 * SB_HEADS * HEAD_DIM
KV_ROWS_FOX = 2 * FOX_HEADS * HEAD_DIM
KV_ROWS_NSA = 2 * NSA_KV_GROUPS * HEAD_DIM
ROW_SB, ROW_FOX, ROW_CMP, ROW_SEL, ROW_WIN = 0, 512, 1024, 1280, 1536
KV_ROWS_ALL = 1792
GATE_LANE = 0
LOGF_LANE = 24

NT = (((1,), (1,)), ((), ()))


def _cparams(sem):
    return pltpu.CompilerParams(dimension_semantics=sem, vmem_limit_bytes=V7X_VMEM_LIMIT_BYTES)


def _rms(x, g):
    return x * lax.rsqrt(jnp.mean(x * x, axis=-1, keepdims=True) + RMS_EPS) * g


def _softplus(z):
    return jnp.maximum(z, 0.0) + jnp.log(1.0 + jnp.exp(-jnp.abs(z)))


def _sigmoid(z):
    return 1.0 / (1.0 + jnp.exp(-z))


def _split_bf16(x, n):
    parts = []
    r = x
    for _ in range(n):
        p = r.astype(BF16)
        parts.append(p)
        r = r - p.astype(F32)
    return parts


def _dot_split_lhs(x, u, n):
    return sum(jnp.dot(p, u, preferred_element_type=F32) for p in _split_bf16(x, n))


def _dot_split_rhs(u, x, n):
    return sum(jnp.dot(u, p, preferred_element_type=F32) for p in _split_bf16(x, n))


def _iota(shape, dim):
    return lax.broadcasted_iota(jnp.int32, shape, dim)


def _stack_heads(q, n_heads):
    lane_head = _iota(q.shape, 1) // HEAD_DIM
    return jnp.concatenate([jnp.where(lane_head == h, q, 0.0) for h in range(n_heads)], axis=0).astype(BF16)


def _stack_nsa(qpad):
    return jnp.concatenate([qpad[:, h * LANES:(h + 1) * LANES] for h in range(NSA_HEADS)], axis=0).astype(BF16)


def _diag_select(o_all, n_heads, tq):
    lane_head = _iota((tq, o_all.shape[1]), 1) // HEAD_DIM
    out = o_all[0:tq]
    for h in range(1, n_heads):
        out = jnp.where(lane_head == h, o_all[h * tq:(h + 1) * tq], out)
    return out


def _row_t(n_rows, tq):
    return _iota((n_rows, 1), 0) % tq


def _nsa_slopes(n_rows, tq):
    head = _iota((n_rows, 1), 0) // tq
    slope = jnp.full((n_rows, 1), 0.5, F32)
    for h in range(1, NSA_HEADS):
        slope = jnp.where(head == h, 2.0 ** -(h + 1), slope)
    return slope


def _flash_update(s, v_t, m_ref, l_ref, acc_ref):
    m_old = m_ref[...]
    m_new = jnp.maximum(m_old, jnp.max(s, axis=1, keepdims=True))
    alpha = jnp.exp(m_old - m_new)
    p = jnp.exp(s - m_new)
    l_ref[...] = alpha * l_ref[...] + jnp.sum(p, axis=1, keepdims=True)
    acc_ref[...] = alpha * acc_ref[...] + lax.dot_general(p.astype(BF16), v_t, NT, preferred_element_type=F32)
    m_ref[...] = m_new


def _flash_init(m_ref, l_ref, acc_ref):
    m_ref[...] = jnp.full(m_ref.shape, NEG_INF, F32)
    l_ref[...] = jnp.zeros(l_ref.shape, F32)
    acc_ref[...] = jnp.zeros(acc_ref.shape, F32)


def _flash_out(l_ref, acc_ref):
    return acc_ref[...] / jnp.maximum(l_ref[...], 1e-30)


def _masked_softmax_rows(s, valid):
    s = jnp.where(valid, s, NEG_INF)
    m = jnp.max(s, axis=1, keepdims=True)
    p = jnp.where(valid, jnp.exp(s - m), 0.0)
    return p / jnp.maximum(jnp.sum(p, axis=1, keepdims=True), 1e-30)


def _top_blocks(score, n_pick):
    nb = score.shape[1]
    blk = _iota(score.shape, 1).astype(F32)
    sel = jnp.zeros(score.shape, F32)
    for _ in range(n_pick):
        m = jnp.max(score, axis=1, keepdims=True)
        first = jnp.min(jnp.where(score == m, blk, float(nb)), axis=1, keepdims=True)
        hit = blk == first
        sel = jnp.where(hit, 1.0, sel)
        score = jnp.where(hit, -jnp.inf, score)
    return sel


def _sb_step(qbd, k_t, v_t, u, valid, carry_ref, acc_ref):
    z = jnp.dot(qbd, k_t, preferred_element_type=F32)
    log_keep = -_softplus(z)
    if valid is not None:
        log_keep = jnp.where(valid, log_keep, 0.0)
    after = _dot_split_lhs(log_keep, u, 2)
    carry = carry_ref[...]
    w = jnp.exp(z + log_keep + after + carry)
    if valid is not None:
        w = jnp.where(valid, w, 0.0)
    acc_ref[...] += lax.dot_general(w.astype(BF16), v_t, NT, preferred_element_type=F32)
    carry_ref[...] = carry + after[:, 0:1] + log_keep[:, 0:1]


def _strict_upper(n):
    return jnp.where(_iota((n, n), 0) > _iota((n, n), 1), 1.0, 0.0).astype(BF16)


def _ffn_kernel(x_ref, pre_ref, post_ref, wgu_ref, wd_ref, o_ref):
    x = x_ref[...]
    h = _rms(x, pre_ref[...]).astype(BF16)
    acc = jnp.zeros(x.shape, F32)
    for c in range(D_FF // FFN_CHUNK):
        lo = c * FFN_CHUNK
        gate = jnp.dot(h, wgu_ref[:, lo:lo + FFN_CHUNK], preferred_element_type=F32)
        up = jnp.dot(h, wgu_ref[:, D_FF + lo:D_FF + lo + FFN_CHUNK], preferred_element_type=F32)
        a = (gate * _sigmoid(gate) * up).astype(BF16)
        acc = acc + jnp.dot(a, wd_ref[lo:lo + FFN_CHUNK, :], preferred_element_type=F32)
    o_ref[...] = x + 0.5 * _rms(acc, post_ref[...])


def _ffn(x, pre_g, post_g, w_gu, w_down, tm):
    m = x.shape[0]
    const = lambda i: (0, 0)
    return pl.pallas_call(
        _ffn_kernel,
        grid=(m // tm,),
        in_specs=[pl.BlockSpec((tm, D_MODEL), lambda i: (i, 0)),
                  pl.BlockSpec((1, D_MODEL), const), pl.BlockSpec((1, D_MODEL), const),
                  pl.BlockSpec((D_MODEL, 2 * D_FF), const, pipeline_mode=pl.Buffered(1)),
                  pl.BlockSpec((D_FF, D_MODEL), const, pipeline_mode=pl.Buffered(1))],
        out_specs=pl.BlockSpec((tm, D_MODEL), lambda i: (i, 0)),
        out_shape=jax.ShapeDtypeStruct((m, D_MODEL), F32),
        compiler_params=_cparams(("arbitrary",)),
        name="half_ffn",
    )(x, pre_g, post_g, w_gu, w_down)


def _proj_kernel(x_ref, g_ref, wq_ref, wkvt_ref, wcmp_ref, wsm_ref, brow_ref, cw_ref,
                 sbq_ref, nsaq_ref, foxq_ref, kvt_ref, kvt3_ref, ck_ref, small_ref, fcum_ref, ft3_ref,
                 carry_ref, *, tm):
    @pl.when(pl.program_id(0) == 0)
    def _():
        carry_ref[...] = jnp.zeros(carry_ref.shape, F32)

    h = _rms(x_ref[...], g_ref[...]).astype(BF16)
    q = jnp.dot(h, wq_ref[...], preferred_element_type=F32)
    sbq_ref[...] = q[:, 0:256]
    nsaq_ref[...] = q[:, 256:1280]
    foxq_ref[...] = q[:, 1280:1536]
    kvt = lax.dot_general(wkvt_ref[...], h, NT, preferred_element_type=F32)
    kvt_ref[...] = kvt
    kvt3_ref[...] = kvt.astype(BF16)
    ycmp = jnp.dot(h, wcmp_ref[...], preferred_element_type=F32)
    ck_ref[...] = (ycmp * cw_ref[...]).reshape(tm // CMP_BLOCK, CMP_BLOCK, KV_ROWS_NSA).sum(axis=1)
    ys = jnp.dot(h, wsm_ref[...], preferred_element_type=F32)
    lane = _iota((1, LANES), 1)
    log_f = -_softplus(-(ys + brow_ref[...]))
    is_f = jnp.logical_and(lane >= LOGF_LANE, lane < LOGF_LANE + FOX_HEADS)
    small_ref[...] = jnp.where(lane < LOGF_LANE, _sigmoid(ys), jnp.where(is_f, log_f, 0.0))
    tri = jnp.where(_iota((tm, tm), 0) >= _iota((tm, tm), 1), 1.0, 0.0).astype(BF16)
    cum = _dot_split_rhs(tri, jnp.where(is_f, log_f, 0.0), 3) + carry_ref[...]
    carry_ref[...] = cum[tm - 1:tm, :]
    fcum_ref[...] = cum
    ft3_ref[...] = cum.T


def _proj(x, g, lw, tm):
    m = x.shape[0]
    n = m // tm
    const = lambda i: (0, 0)
    row = lambda i: (i, 0)
    outs = pl.pallas_call(
        functools.partial(_proj_kernel, tm=tm),
        grid=(n,),
        in_specs=[pl.BlockSpec((tm, D_MODEL), row), pl.BlockSpec((1, D_MODEL), const),
                  pl.BlockSpec((D_MODEL, 1536), const), pl.BlockSpec((KV_ROWS_ALL, D_MODEL), const),
                  pl.BlockSpec((D_MODEL, KV_ROWS_NSA), const), pl.BlockSpec((D_MODEL, LANES), const),
                  pl.BlockSpec((1, LANES), const), pl.BlockSpec((tm, KV_ROWS_NSA), const)],
        out_specs=[pl.BlockSpec((tm, 256), row), pl.BlockSpec((tm, 1024), row), pl.BlockSpec((tm, 256), row),
                   pl.BlockSpec((KV_ROWS_ALL, tm), lambda i: (0, i)),
                   pl.BlockSpec((None, KV_ROWS_ALL, tm), lambda i: (i, 0, 0)),
                   pl.BlockSpec((None, tm // CMP_BLOCK, KV_ROWS_NSA), lambda i: (i, 0, 0)),
                   pl.BlockSpec((tm, LANES), row), pl.BlockSpec((tm, LANES), row),
                   pl.BlockSpec((None, LANES, tm), lambda i: (i, 0, 0))],
        out_shape=[jax.ShapeDtypeStruct((m, 256), F32), jax.ShapeDtypeStruct((m, 1024), F32),
                   jax.ShapeDtypeStruct((m, 256), F32),
                   jax.ShapeDtypeStruct((KV_ROWS_ALL, m), F32),
                   jax.ShapeDtypeStruct((n, KV_ROWS_ALL, tm), BF16),
                   jax.ShapeDtypeStruct((n, tm // CMP_BLOCK, KV_ROWS_NSA), F32),
                   jax.ShapeDtypeStruct((m, LANES), F32), jax.ShapeDtypeStruct((m, LANES), F32),
                   jax.ShapeDtypeStruct((n, LANES, tm), F32)],
        scratch_shapes=[pltpu.VMEM((1, LANES), F32)],
        compiler_params=_cparams(("arbitrary",)),
        name="mixer_proj",
    )(x, g, lw["wq"], lw["wkvt"], lw["wcmp"], lw["wsm"], lw["brow"], lw["cw"])
    keys = ("sbq", "nsaq", "foxq", "kvt", "kvt3", "ck", "small", "fcum", "ft3")
    return dict(zip(keys, outs))


def _merge_kernel(x_ref, osb_ref, onsa_ref, ofox_ref, pre_ref, post_ref, wg_ref, wsb_ref, wnsa_ref, wfox_ref,
                  wout_ref, o_ref):
    x = x_ref[...]
    h = _rms(x, pre_ref[...]).astype(BF16)
    mixed = jnp.zeros(x.shape, F32)
    for b, (o_b, w_b) in enumerate(((osb_ref, wsb_ref), (onsa_ref, wnsa_ref), (ofox_ref, wfox_ref))):
        gate = _sigmoid(jnp.dot(h, wg_ref[:, b * D_MODEL:(b + 1) * D_MODEL], preferred_element_type=F32))
        mixed = mixed + gate * jnp.dot(o_b[...].astype(BF16), w_b[...], preferred_element_type=F32)
    y = jnp.dot(mixed.astype(BF16), wout_ref[...], preferred_element_type=F32)
    o_ref[...] = x + _rms(y, post_ref[...])


def _merge(x, o_sb, o_nsa, o_fox, pre_g, post_g, lw, tm):
    m = x.shape[0]
    const = lambda i: (0, 0)
    row = lambda i: (i, 0)
    return pl.pallas_call(
        _merge_kernel,
        grid=(m // tm,),
        in_specs=[pl.BlockSpec((tm, D_MODEL), row), pl.BlockSpec((tm, 256), row), pl.BlockSpec((tm, 1024), row),
                  pl.BlockSpec((tm, 256), row), pl.BlockSpec((1, D_MODEL), const), pl.BlockSpec((1, D_MODEL), const),
                  pl.BlockSpec((D_MODEL, 3 * D_MODEL), const), pl.BlockSpec((256, D_MODEL), const),
                  pl.BlockSpec((1024, D_MODEL), const), pl.BlockSpec((256, D_MODEL), const),
                  pl.BlockSpec((D_MODEL, D_MODEL), const)],
        out_specs=pl.BlockSpec((tm, D_MODEL), row),
        out_shape=jax.ShapeDtypeStruct((m, D_MODEL), F32),
        compiler_params=_cparams(("arbitrary",)),
        name="mixer_merge",
    )(x, o_sb, o_nsa, o_fox, pre_g, post_g, lw["wgate"], lw["wbr_sb"], lw["wbr_nsa"], lw["wbr_fox"], lw["wout"])


def _sb_prompt_kernel(q_ref, kv_ref, o_ref, carry_ref, acc_ref, *, tq, tk):
    i = pl.program_id(0)
    n_rows = SB_HEADS * tq
    qbd = _stack_heads(q_ref[...], SB_HEADS)
    u = _strict_upper(tk)
    kb0 = (i * tq) // tk
    t_pos = i * tq + _row_t(n_rows, tq)
    carry_ref[...] = jnp.zeros(carry_ref.shape, F32)
    acc_ref[...] = jnp.zeros(acc_ref.shape, F32)

    def tile(kb, diagonal):
        k_t = kv_ref[kb, 0:256, :]
        v_t = kv_ref[kb, 256:512, :]
        valid = (kb * tk + _iota((1, tk), 1) < t_pos) if diagonal else None
        _sb_step(qbd, k_t, v_t, u, valid, carry_ref, acc_ref)

    tile(kb0, True)

    def cond(c):
        return jnp.logical_and(c[0] >= 0, c[1] > -SB_EXIT)

    def body(c):
        tile(c[0], False)
        return c[0] - 1, jnp.max(carry_ref[...])

    lax.while_loop(cond, body, (kb0 - 1, jnp.max(carry_ref[...])))
    o_ref[...] = _diag_select(acc_ref[...], SB_HEADS, tq).astype(o_ref.dtype)


def _sb_prompt(sbq, kvt3, tq):
    t = sbq.shape[0]
    n_kt, _, tk = kvt3.shape
    return pl.pallas_call(
        functools.partial(_sb_prompt_kernel, tq=tq, tk=tk),
        grid=(t // tq,),
        in_specs=[pl.BlockSpec((tq, 256), lambda i: (i, 0)),
                  pl.BlockSpec((n_kt, KV_ROWS_SB, tk), lambda i: (0, ROW_SB // KV_ROWS_SB, 0))],
        out_specs=pl.BlockSpec((tq, 256), lambda i: (i, 0)),
        out_shape=jax.ShapeDtypeStruct((t, 256), BF16),
        scratch_shapes=[pltpu.VMEM((SB_HEADS * tq, 1), F32), pltpu.VMEM((SB_HEADS * tq, 256), F32)],
        compiler_params=_cparams(("arbitrary",)),
        name="sb_prompt",
    )(sbq, kvt3)


def _fox_prompt_kernel(q_ref, kv_ref, fq_ref, ft_ref, o_ref, m_ref, l_ref, acc_ref, *, tq, tk):
    i = pl.program_id(0)
    n_rows = FOX_HEADS * tq
    qbd = _stack_heads(q_ref[...], FOX_HEADS)
    fq = jnp.concatenate([fq_ref[:, LOGF_LANE + h:LOGF_LANE + h + 1] for h in range(FOX_HEADS)], axis=0)
    kb0 = (i * tq) // tk
    t_pos = i * tq + _row_t(n_rows, tq)
    _flash_init(m_ref, l_ref, acc_ref)

    def tile(kb, diagonal):
        k_t = kv_ref[kb, 0:256, :]
        v_t = kv_ref[kb, 256:512, :]
        f_t = ft_ref[kb]
        fk = jnp.concatenate([jnp.broadcast_to(f_t[LOGF_LANE + h:LOGF_LANE + h + 1, :], (tq, tk))
                              for h in range(FOX_HEADS)], axis=0)
        s = jnp.dot(qbd, k_t, preferred_element_type=F32) + (fq - fk)
        if diagonal:
            s = jnp.where(kb * tk + _iota((1, tk), 1) <= t_pos, s, NEG_INF)
        _flash_update(s, v_t, m_ref, l_ref, acc_ref)

    def body(kb, c):
        tile(kb, False)
        return c

    lax.fori_loop(0, kb0, body, 0)
    tile(kb0, True)
    o_ref[...] = _diag_select(_flash_out(l_ref, acc_ref), FOX_HEADS, tq).astype(o_ref.dtype)


def _fox_prompt(foxq, kvt3, fcum, ft3, tq):
    t = foxq.shape[0]
    n_kt, _, tk = kvt3.shape
    n_rows = FOX_HEADS * tq
    return pl.pallas_call(
        functools.partial(_fox_prompt_kernel, tq=tq, tk=tk),
        grid=(t // tq,),
        in_specs=[pl.BlockSpec((tq, 256), lambda i: (i, 0)),
                  pl.BlockSpec((n_kt, KV_ROWS_FOX, tk), lambda i: (0, ROW_FOX // KV_ROWS_FOX, 0)),
                  pl.BlockSpec((tq, LANES), lambda i: (i, 0)),
                  pl.BlockSpec((n_kt, LANES, tk), lambda i: (0, 0, 0))],
        out_specs=pl.BlockSpec((tq, 256), lambda i: (i, 0)),
        out_shape=jax.ShapeDtypeStruct((t, 256), BF16),
        scratch_shapes=[pltpu.VMEM((n_rows, 1), F32), pltpu.VMEM((n_rows, 1), F32), pltpu.VMEM((n_rows, 256), F32)],
        compiler_params=_cparams(("arbitrary",)),
        name="fox_prompt",
    )(foxq, kvt3, fcum, ft3)


def _nsa_combine(o_cmp, o_sel, o_win, gates, tq, o_ref):
    for h in range(NSA_HEADS):
        rows = slice(h * tq, (h + 1) * tq)
        o = (gates[:, GATE_LANE + h:GATE_LANE + h + 1] * o_cmp[rows]
             + gates[:, GATE_LANE + NSA_HEADS + h:GATE_LANE + NSA_HEADS + h + 1] * o_sel[rows]
             + gates[:, GATE_LANE + 2 * NSA_HEADS + h:GATE_LANE + 2 * NSA_HEADS + h + 1] * o_win[rows])
        o_ref[:, h * LANES:(h + 1) * LANES] = o.astype(o_ref.dtype)


def _group_rows(x_g0, x_g1):
    return jnp.concatenate([x_g0] * NSA_GROUP_SIZE + [x_g1] * NSA_GROUP_SIZE, axis=0)


def _nsa_prompt_kernel(q_ref, ckt_ref, sel_ref, win_ref, gate_ref, o_ref, m_ref, l_ref, acc_ref, *, tq, tk, n_blocks):
    i = pl.program_id(0)
    n_rows = NSA_HEADS * tq
    qbd = _stack_nsa(q_ref[...])
    slope = _nsa_slopes(n_rows, tq)
    t_pos = i * tq + _row_t(n_rows, tq)
    t_pos_f = t_pos.astype(F32)
    kb0 = (i * tq) // tk

    ckt = ckt_ref[...]
    cmp_end = _iota((1, n_blocks), 1) * CMP_BLOCK + (CMP_BLOCK - 1)
    s_c = jnp.dot(qbd, ckt[0:128].astype(BF16), preferred_element_type=F32)
    s_c = s_c - slope * (t_pos_f - cmp_end.astype(F32))
    p_c = _masked_softmax_rows(s_c, cmp_end <= t_pos)
    o_cmp = lax.dot_general(p_c.astype(BF16), ckt[128:256].astype(BF16), NT, preferred_element_type=F32)

    t_q = i * tq + _iota((tq, 1), 0)
    cur = t_q // SEL_BLOCK
    blk = _iota((tq, n_blocks), 1)
    forced = jnp.logical_or(jnp.logical_or(blk == cur, blk == cur - 1), blk == 0)
    sels = []
    for g in range(NSA_KV_GROUPS):
        score = sum(p_c[(g * NSA_GROUP_SIZE + r) * tq:(g * NSA_GROUP_SIZE + r + 1) * tq] for r in range(NSA_GROUP_SIZE))
        score = jnp.where(forced, FORCE_SCORE, jnp.where(blk > cur, -FORCE_SCORE, score))
        sels.append(_top_blocks(score, min(SEL_TOPK, n_blocks)).astype(BF16))

    _flash_init(m_ref, l_ref, acc_ref)
    blocks_per_tile = tk // SEL_BLOCK

    def sel_tile(kb, c):
        tile = sel_ref[kb]
        expand = jnp.where(_iota((n_blocks, tk), 0) == kb * blocks_per_tile + _iota((n_blocks, tk), 1) // SEL_BLOCK,
                           1.0, 0.0).astype(BF16)
        chosen = _group_rows(jnp.dot(sels[0], expand, preferred_element_type=F32),
                             jnp.dot(sels[1], expand, preferred_element_type=F32))
        dist = t_pos_f - (kb * tk + _iota((1, tk), 1)).astype(F32)
        s = jnp.dot(qbd, tile[0:128], preferred_element_type=F32) - slope * dist
        s = jnp.where(dist >= 0.0, jnp.where(chosen > 0.5, s, NEG_INF), NEG_INF)
        _flash_update(s, tile[128:256], m_ref, l_ref, acc_ref)
        return c

    lax.fori_loop(0, kb0 + 1, sel_tile, 0)
    o_sel = _flash_out(l_ref, acc_ref)

    _flash_init(m_ref, l_ref, acc_ref)

    def win_tile(kb, c):
        tile = win_ref[kb]
        dist = t_pos_f - (kb * tk + _iota((1, tk), 1)).astype(F32)
        s = jnp.dot(qbd, tile[0:128], preferred_element_type=F32) - slope * dist
        s = jnp.where(dist >= 0.0, jnp.where(dist < float(WINDOW), s, NEG_INF), NEG_INF)
        _flash_update(s, tile[128:256], m_ref, l_ref, acc_ref)
        return c

    lax.fori_loop(jnp.maximum(i * tq - (WINDOW - 1), 0) // tk, kb0 + 1, win_tile, 0)
    o_win = _flash_out(l_ref, acc_ref)
    _nsa_combine(o_cmp, o_sel, o_win, gate_ref[...], tq, o_ref)


def _nsa_prompt(nsaq, ckt, kvt3, small, tq):
    t = nsaq.shape[0]
    n_kt, _, tk = kvt3.shape
    n_blocks = ckt.shape[1]
    n_rows = NSA_HEADS * tq
    return pl.pallas_call(
        functools.partial(_nsa_prompt_kernel, tq=tq, tk=tk, n_blocks=n_blocks),
        grid=(t // tq,),
        in_specs=[pl.BlockSpec((tq, 1024), lambda i: (i, 0)),
                  pl.BlockSpec((KV_ROWS_NSA, n_blocks), lambda i: (0, 0)),
                  pl.BlockSpec((n_kt, KV_ROWS_NSA, tk), lambda i: (0, ROW_SEL // KV_ROWS_NSA, 0)),
                  pl.BlockSpec((n_kt, KV_ROWS_NSA, tk), lambda i: (0, ROW_WIN // KV_ROWS_NSA, 0)),
                  pl.BlockSpec((tq, LANES), lambda i: (i, 0))],
        out_specs=pl.BlockSpec((tq, 1024), lambda i: (i, 0)),
        out_shape=jax.ShapeDtypeStruct((t, 1024), BF16),
        scratch_shapes=[pltpu.VMEM((n_rows, 1), F32), pltpu.VMEM((n_rows, 1), F32), pltpu.VMEM((n_rows, LANES), F32)],
        compiler_params=_cparams(("arbitrary",)),
        name="nsa_prompt",
    )(nsaq, ckt, kvt3, kvt3, small)


def _page_specs(rows, n_pages, layer, reverse):
    def spec(j):
        def index(b, s, pt):
            slot = s * PAGES_PER_STEP + j
            if reverse:
                slot = n_pages - 1 - slot
            return (layer, pt[b, slot], 0, 0)
        return pl.BlockSpec((None, None, rows, PAGE_SIZE), index)
    return [spec(j) for j in range(PAGES_PER_STEP)]


def _sb_sample_kernel(pt_ref, q_ref, new_ref, *rest, n_new):
    pages = rest[:PAGES_PER_STEP]
    o_ref, carry_ref, acc_ref = rest[PAGES_PER_STEP:]
    s = pl.program_id(1)
    n_rows = SB_HEADS * n_new
    qbd = _stack_heads(q_ref[...], SB_HEADS)
    u = _strict_upper(PAGE_SIZE)

    @pl.when(s == 0)
    def _():
        carry_ref[...] = jnp.zeros(carry_ref.shape, F32)
        acc_ref[...] = jnp.zeros(acc_ref.shape, F32)
        new = new_ref[...].astype(BF16)
        valid = _iota((1, PAGE_SIZE), 1) < _row_t(n_rows, n_new)
        _sb_step(qbd, new[0:256], new[256:512], u, valid, carry_ref, acc_ref)

    for page in pages:
        @pl.when(jnp.max(carry_ref[...]) > -SB_EXIT)
        def _():
            kv = page[...].astype(BF16)
            _sb_step(qbd, kv[0:256], kv[256:512], u, None, carry_ref, acc_ref)

    @pl.when(s == pl.num_programs(1) - 1)
    def _():
        o_ref[...] = _diag_select(acc_ref[...], SB_HEADS, n_new)


def _sb_sample(q, new_t, cache_t, page_table, layer, n_new):
    n_seq, n_pages = page_table.shape
    n_rows = SB_HEADS * n_new
    grid_spec = pltpu.PrefetchScalarGridSpec(
        num_scalar_prefetch=1, grid=(n_seq, n_pages // PAGES_PER_STEP),
        in_specs=[pl.BlockSpec((n_new, 256), lambda b, s, pt: (b, 0)),
                  pl.BlockSpec((None, KV_ROWS_SB, PAGE_SIZE), lambda b, s, pt: (b, 0, 0))]
        + _page_specs(KV_ROWS_SB, n_pages, layer, True),
        out_specs=pl.BlockSpec((n_new, 256), lambda b, s, pt: (b, 0)),
        scratch_shapes=[pltpu.VMEM((n_rows, 1), F32), pltpu.VMEM((n_rows, 256), F32)])
    return pl.pallas_call(
        functools.partial(_sb_sample_kernel, n_new=n_new), grid_spec=grid_spec,
        out_shape=jax.ShapeDtypeStruct((n_seq * n_new, 256), F32),
        compiler_params=_cparams(("arbitrary", "arbitrary")), name="sb_sample",
    )(page_table, q, new_t, *([cache_t] * PAGES_PER_STEP))


def _expand_heads(lf, n_new):
    n = lf.shape[1]
    return jnp.concatenate([jnp.broadcast_to(lf[h:h + 1, :], (n_new, n)) for h in range(FOX_HEADS)], axis=0)


def _fox_sample_kernel(pt_ref, q_ref, new_ref, lfnew_ref, *rest, n_new):
    pages = rest[:PAGES_PER_STEP]
    lf_pages = rest[PAGES_PER_STEP:2 * PAGES_PER_STEP]
    o_ref, m_ref, l_ref, acc_ref, sfx_ref, cnew_ref = rest[2 * PAGES_PER_STEP:]
    s = pl.program_id(1)
    n_rows = FOX_HEADS * n_new
    qbd = _stack_heads(q_ref[...], FOX_HEADS)
    u = _strict_upper(PAGE_SIZE)
    lane = _iota((1, PAGE_SIZE), 1)
    t_row = _row_t(n_rows, n_new)

    @pl.when(s == 0)
    def _():
        _flash_init(m_ref, l_ref, acc_ref)
        sfx_ref[...] = jnp.zeros(sfx_ref.shape, F32)
        lf = _expand_heads(lfnew_ref[...], n_new)
        incl = jnp.where(_iota((PAGE_SIZE, PAGE_SIZE), 0) <= _iota((PAGE_SIZE, PAGE_SIZE), 1), 1.0, 0.0).astype(BF16)
        cum = _dot_split_lhs(lf, incl, 3)
        c_t = jnp.sum(jnp.where(lane == t_row, cum, 0.0), axis=1, keepdims=True)
        cnew_ref[...] = c_t
        new = new_ref[...].astype(BF16)
        sc = jnp.dot(qbd, new[0:256], preferred_element_type=F32) + (c_t - cum)
        sc = jnp.where(lane <= t_row, sc, NEG_INF)
        _flash_update(sc, new[256:512], m_ref, l_ref, acc_ref)

    for page, lf_page in zip(pages, lf_pages):
        kv = page[...].astype(BF16)
        lf = _expand_heads(lf_page[...], n_new)
        inner = _dot_split_lhs(lf, u, 3)
        sfx = sfx_ref[...]
        sc = jnp.dot(qbd, kv[0:256], preferred_element_type=F32) + (inner + (sfx + cnew_ref[...]))
        _flash_update(sc, kv[256:512], m_ref, l_ref, acc_ref)
        sfx_ref[...] = sfx + inner[:, 0:1] + lf[:, 0:1]

    @pl.when(s == pl.num_programs(1) - 1)
    def _():
        o_ref[...] = _diag_select(_flash_out(l_ref, acc_ref), FOX_HEADS, n_new)


def _fox_sample(q, new_t, lf_new, cache_t, lf_cache_t, page_table, layer, n_new):
    n_seq, n_pages = page_table.shape
    n_rows = FOX_HEADS * n_new
    grid_spec = pltpu.PrefetchScalarGridSpec(
        num_scalar_prefetch=1, grid=(n_seq, n_pages // PAGES_PER_STEP),
        in_specs=[pl.BlockSpec((n_new, 256), lambda b, s, pt: (b, 0)),
                  pl.BlockSpec((None, KV_ROWS_FOX, PAGE_SIZE), lambda b, s, pt: (b, 0, 0)),
                  pl.BlockSpec((None, FOX_HEADS, PAGE_SIZE), lambda b, s, pt: (b, 0, 0))]
        + _page_specs(KV_ROWS_FOX, n_pages, layer, True) + _page_specs(FOX_HEADS, n_pages, layer, True),
        out_specs=pl.BlockSpec((n_new, 256), lambda b, s, pt: (b, 0)),
        scratch_shapes=[pltpu.VMEM((n_rows, 1), F32), pltpu.VMEM((n_rows, 1), F32), pltpu.VMEM((n_rows, 256), F32),
                        pltpu.VMEM((n_rows, 1), F32), pltpu.VMEM((n_rows, 1), F32)])
    return pl.pallas_call(
        functools.partial(_fox_sample_kernel, n_new=n_new), grid_spec=grid_spec,
        out_shape=jax.ShapeDtypeStruct((n_seq * n_new, 256), F32),
        compiler_params=_cparams(("arbitrary", "arbitrary")), name="fox_sample",
    )(page_table, q, new_t, lf_new, *([cache_t] * PAGES_PER_STEP), *([lf_cache_t] * PAGES_PER_STEP))


def _write_head_padded(o_all, n_new, o_ref):
    for h in range(NSA_HEADS):
        o_ref[:, h * LANES:(h + 1) * LANES] = o_all[h * n_new:(h + 1) * n_new]


def _cmp_sample_kernel(pt_ref, q_ref, cw_ref, *rest, n_new, past_len):
    pages = rest[:PAGES_PER_STEP]
    o_ref, sel_ref, ckt_ref = rest[PAGES_PER_STEP:]
    s = pl.program_id(1)
    n_rows = NSA_HEADS * n_new
    n_blocks = ckt_ref.shape[1]
    blocks_per_page = PAGE_SIZE // CMP_BLOCK
    lane = _iota((1, PAGE_SIZE), 1)
    blk_lane = _iota((1, n_blocks), 1)

    @pl.when(s == 0)
    def _():
        ckt_ref[...] = jnp.zeros(ckt_ref.shape, F32)

    ckt = ckt_ref[...]
    for j, page in enumerate(pages):
        prod = page[...] * cw_ref[...]
        first = (s * PAGES_PER_STEP + j) * blocks_per_page
        for c in range(blocks_per_page):
            col = jnp.sum(jnp.where(lane // CMP_BLOCK == c, prod, 0.0), axis=1, keepdims=True)
            ckt = jnp.where(blk_lane == first + c, col, ckt)
    ckt_ref[...] = ckt

    @pl.when(s == pl.num_programs(1) - 1)
    def _():
        qbd = _stack_nsa(q_ref[...])
        slope = _nsa_slopes(n_rows, n_new)
        t_pos = past_len + _row_t(n_rows, n_new)
        cmp_end = blk_lane * CMP_BLOCK + (CMP_BLOCK - 1)
        ck = ckt_ref[...]
        s_c = jnp.dot(qbd, ck[0:128].astype(BF16), preferred_element_type=F32)
        s_c = s_c - slope * (t_pos - cmp_end).astype(F32)
        p_c = _masked_softmax_rows(s_c, cmp_end <= t_pos)
        o_cmp = lax.dot_general(p_c.astype(BF16), ck[128:256].astype(BF16), NT, preferred_element_type=F32)
        _write_head_padded(o_cmp, n_new, o_ref)
        cur = (past_len + _iota((n_new, 1), 0)) // SEL_BLOCK
        blk = _iota((n_new, n_blocks), 1)
        forced = jnp.logical_or(blk == cur - 1, blk == 0)
        for g in range(NSA_KV_GROUPS):
            score = sum(p_c[(g * NSA_GROUP_SIZE + r) * n_new:(g * NSA_GROUP_SIZE + r + 1) * n_new]
                        for r in range(NSA_GROUP_SIZE))
            score = jnp.where(forced, FORCE_SCORE, jnp.where(blk > cur, -FORCE_SCORE, score))
            sel_ref[g * n_new:(g + 1) * n_new, :] = _top_blocks(score, min(SEL_TOPK - 1, n_blocks))


def _cmp_sample(nsaq, cw_row, cache_t, page_table, layer, n_new, past_len):
    n_seq, n_pages = page_table.shape
    n_blocks = past_len // CMP_BLOCK
    grid_spec = pltpu.PrefetchScalarGridSpec(
        num_scalar_prefetch=1, grid=(n_seq, n_pages // PAGES_PER_STEP),
        in_specs=[pl.BlockSpec((n_new, 1024), lambda b, s, pt: (b, 0)),
                  pl.BlockSpec((1, PAGE_SIZE), lambda b, s, pt: (0, 0))]
        + _page_specs(KV_ROWS_NSA, n_pages, layer, False),
        out_specs=[pl.BlockSpec((n_new, 1024), lambda b, s, pt: (b, 0)),
                   pl.BlockSpec((None, NSA_KV_GROUPS * n_new, n_blocks), lambda b, s, pt: (b, 0, 0))],
        scratch_shapes=[pltpu.VMEM((KV_ROWS_NSA, n_blocks), F32)])
    return pl.pallas_call(
        functools.partial(_cmp_sample_kernel, n_new=n_new, past_len=past_len), grid_spec=grid_spec,
        out_shape=[jax.ShapeDtypeStruct((n_seq * n_new, 1024), F32),
                   jax.ShapeDtypeStruct((n_seq, NSA_KV_GROUPS * n_new, n_blocks), F32)],
        compiler_params=_cparams(("arbitrary", "arbitrary")), name="cmp_sample",
    )(page_table, nsaq, cw_row, *([cache_t] * PAGES_PER_STEP))


def _sel_sample_kernel(pt_ref, q_ref, new_ref, sel_ref, *rest, n_new, past_len):
    pages = rest[:PAGES_PER_STEP]
    o_ref, m_ref, l_ref, acc_ref = rest[PAGES_PER_STEP:]
    s = pl.program_id(1)
    n_rows = NSA_HEADS * n_new
    n_blocks = sel_ref.shape[1]
    qbd = _stack_nsa(q_ref[...])
    slope = _nsa_slopes(n_rows, n_new)
    t_row = _row_t(n_rows, n_new)
    lane = _iota((1, PAGE_SIZE), 1)
    blocks_per_page = PAGE_SIZE // SEL_BLOCK

    @pl.when(s == 0)
    def _():
        _flash_init(m_ref, l_ref, acc_ref)
        new = new_ref[...].astype(BF16)
        dist = (t_row - lane).astype(F32)
        sc = jnp.dot(qbd, new[0:128], preferred_element_type=F32) - slope * dist
        _flash_update(jnp.where(dist >= 0.0, sc, NEG_INF), new[128:256], m_ref, l_ref, acc_ref)

    sel = sel_ref[...].astype(BF16)
    for j, page in enumerate(pages):
        slot = s * PAGES_PER_STEP + j
        kv = page[...].astype(BF16)
        expand = jnp.where(_iota((n_blocks, PAGE_SIZE), 0)
                           == slot * blocks_per_page + _iota((n_blocks, PAGE_SIZE), 1) // SEL_BLOCK, 1.0, 0.0).astype(BF16)
        ch = jnp.dot(sel, expand, preferred_element_type=F32)
        chosen = _group_rows(ch[0:n_new], ch[n_new:2 * n_new])
        dist = (past_len + t_row - (slot * PAGE_SIZE + lane)).astype(F32)
        sc = jnp.dot(qbd, kv[0:128], preferred_element_type=F32) - slope * dist
        _flash_update(jnp.where(chosen > 0.5, sc, NEG_INF), kv[128:256], m_ref, l_ref, acc_ref)

    @pl.when(s == pl.num_programs(1) - 1)
    def _():
        _write_head_padded(_flash_out(l_ref, acc_ref), n_new, o_ref)


def _sel_sample(nsaq, new_t, sel, cache_t, page_table, layer, n_new, past_len):
    n_seq, n_pages = page_table.shape
    n_rows = NSA_HEADS * n_new
    n_blocks = sel.shape[2]
    grid_spec = pltpu.PrefetchScalarGridSpec(
        num_scalar_prefetch=1, grid=(n_seq, n_pages // PAGES_PER_STEP),
        in_specs=[pl.BlockSpec((n_new, 1024), lambda b, s, pt: (b, 0)),
                  pl.BlockSpec((None, KV_ROWS_NSA, PAGE_SIZE), lambda b, s, pt: (b, 0, 0)),
                  pl.BlockSpec((None, NSA_KV_GROUPS * n_new, n_blocks), lambda b, s, pt: (b, 0, 0))]
        + _page_specs(KV_ROWS_NSA, n_pages, layer, False),
        out_specs=pl.BlockSpec((n_new, 1024), lambda b, s, pt: (b, 0)),
        scratch_shapes=[pltpu.VMEM((n_rows, 1), F32), pltpu.VMEM((n_rows, 1), F32), pltpu.VMEM((n_rows, LANES), F32)])
    return pl.pallas_call(
        functools.partial(_sel_sample_kernel, n_new=n_new, past_len=past_len), grid_spec=grid_spec,
        out_shape=jax.ShapeDtypeStruct((n_seq * n_new, 1024), F32),
        compiler_params=_cparams(("arbitrary", "arbitrary")), name="sel_sample",
    )(page_table, nsaq, new_t, sel, *([cache_t] * PAGES_PER_STEP))


def _win_sample_kernel(q_ref, new_ref, win_ref, ocmp_ref, osel_ref, gate_ref, o_ref, *, n_new):
    n_rows = NSA_HEADS * n_new
    wb = win_ref.shape[1]
    qbd = _stack_nsa(q_ref[...])
    slope = _nsa_slopes(n_rows, n_new)
    t_row = _row_t(n_rows, n_new)
    kv = jnp.concatenate([win_ref[...], new_ref[...]], axis=1).astype(BF16)
    dist = (wb + t_row - _iota((1, wb + PAGE_SIZE), 1)).astype(F32)
    sc = jnp.dot(qbd, kv[0:128], preferred_element_type=F32) - slope * dist
    valid = jnp.logical_and(dist >= 0.0, dist < float(WINDOW))
    p = _masked_softmax_rows(sc, valid)
    o_win = lax.dot_general(p.astype(BF16), kv[128:256], NT, preferred_element_type=F32)
    gates = gate_ref[...]
    for h in range(NSA_HEADS):
        cols = slice(h * LANES, (h + 1) * LANES)
        o_ref[:, cols] = (gates[:, GATE_LANE + h:GATE_LANE + h + 1] * ocmp_ref[:, cols]
                          + gates[:, GATE_LANE + NSA_HEADS + h:GATE_LANE + NSA_HEADS + h + 1] * osel_ref[:, cols]
                          + gates[:, GATE_LANE + 2 * NSA_HEADS + h:GATE_LANE + 2 * NSA_HEADS + h + 1]
                          * o_win[h * n_new:(h + 1) * n_new])


def _win_sample(nsaq, new_t, win_t, o_cmp, o_sel, small, n_new):
    n_seq, _, wb = win_t.shape
    row = lambda b: (b, 0)
    return pl.pallas_call(
        functools.partial(_win_sample_kernel, n_new=n_new),
        grid=(n_seq,),
        in_specs=[pl.BlockSpec((n_new, 1024), row),
                  pl.BlockSpec((None, KV_ROWS_NSA, PAGE_SIZE), lambda b: (b, 0, 0)),
                  pl.BlockSpec((None, KV_ROWS_NSA, wb), lambda b: (b, 0, 0)),
                  pl.BlockSpec((n_new, 1024), row), pl.BlockSpec((n_new, 1024), row),
                  pl.BlockSpec((n_new, LANES), row)],
        out_specs=pl.BlockSpec((n_new, 1024), row),
        out_shape=jax.ShapeDtypeStruct((n_seq * n_new, 1024), F32),
        compiler_params=_cparams(("arbitrary",)), name="win_sample",
    )(nsaq, new_t, win_t, o_cmp, o_sel, small)


def _layer_weights(l, w_in, b_forget, cmp_w, w_br_sb, w_br_nsa, w_br_fox, w_out, tm):
    w = w_in[l]
    scale = HEAD_DIM ** -0.5
    zeros64 = jnp.zeros((D_MODEL, HEAD_DIM), F32)
    nsa_q = w[:, 768:1280] * scale
    pads = []
    for h in range(NSA_HEADS):
        qh = nsa_q[:, h * HEAD_DIM:(h + 1) * HEAD_DIM]
        pads += [qh, zeros64] if h < NSA_GROUP_SIZE else [zeros64, qh]
    wq = jnp.concatenate([w[:, 0:256] * scale] + pads + [w[:, 2072:2328] * scale], axis=1)
    wkv = jnp.concatenate([w[:, 256:768], w[:, 2328:2840], w[:, 1280:1536], w[:, 1536:1792], w[:, 1792:2048]], axis=1)
    wsm = jnp.concatenate([w[:, 2048:2072], w[:, 2840:2844], jnp.zeros((D_MODEL, LANES - 28), F32)], axis=1)
    brow = jnp.zeros((1, LANES), F32).at[0, LOGF_LANE:LOGF_LANE + FOX_HEADS].set(b_forget[l])
    wn = w_br_nsa[l].reshape(NSA_HEADS, HEAD_DIM, D_MODEL)
    zrow = jnp.zeros((HEAD_DIM, D_MODEL), F32)
    rows = []
    for h in range(NSA_HEADS):
        rows += [wn[h], zrow] if h < NSA_GROUP_SIZE else [zrow, wn[h]]
    return dict(
        wq=wq.astype(BF16), wkvt=wkv.T.astype(BF16), wcmp=w[:, 1280:1536].astype(BF16), wsm=wsm.astype(BF16),
        brow=brow, cw=jnp.broadcast_to(jnp.tile(cmp_w[l], tm // CMP_BLOCK)[:, None], (tm, KV_ROWS_NSA)),
        cw_row=jnp.tile(cmp_w[l], PAGE_SIZE // CMP_BLOCK)[None, :],
        wgate=w[:, 2844:5916].astype(BF16), wbr_sb=w_br_sb[l].astype(BF16),
        wbr_nsa=jnp.concatenate(rows, axis=0).astype(BF16), wbr_fox=w_br_fox[l].astype(BF16),
        wout=w_out[l].astype(BF16))


def _cache_t(cache):
    d, n_pool, page = cache.shape[:3]
    return jnp.transpose(cache, (0, 1, 3, 4, 5, 2)).reshape(d, n_pool, -1, page)


def _new_pages(kvt_rows, n_seq, n_new):
    r = kvt_rows.shape[0]
    x = jnp.transpose(kvt_rows.reshape(r, n_seq, n_new), (1, 0, 2))
    return jnp.pad(x, ((0, 0), (0, 0), (0, PAGE_SIZE - n_new)))


def _kv_out(kvt_rows, lead_shape, heads):
    n = kvt_rows.shape[1]
    return jnp.transpose(kvt_rows.reshape(2, heads, HEAD_DIM, n), (3, 0, 1, 2)).reshape(*lead_shape, 2, heads, HEAD_DIM)


def kernel(x_prompt, x_sample, cache_sb_kv, cache_cmp_kv, cache_sel_kv, cache_fox_kv, cache_fox_logf, state_win_kv, page_table, ffn1_pre_g, ffn1_post_g, ffn1_w_gu, ffn1_w_down, mix_pre_g, mix_post_g, w_in, b_forget, cmp_w, w_br_sb, w_br_nsa, w_br_fox, w_out, ffn2_pre_g, ffn2_post_g, ffn2_w_gu, ffn2_w_down):
    depth = w_in.shape[0]
    batch, seq, _ = x_prompt.shape
    n_seq, n_new, _ = x_sample.shape
    past_len = page_table.shape[1] * PAGE_SIZE
    assert batch == 1 and seq % ATT_TILE == 0 and seq >= WINDOW + ATT_TILE
    assert past_len % SEL_BLOCK == 0 and n_new <= CMP_BLOCK and page_table.shape[1] % PAGES_PER_STEP == 0
    m_s = n_seq * n_new
    tm_s = m_s if m_s <= ATT_TILE else ATT_TILE
    assert m_s % tm_s == 0 and tm_s % CMP_BLOCK == 0

    x_p = x_prompt.reshape(seq, D_MODEL)
    x_s = x_sample.reshape(m_s, D_MODEL)
    sb_c, cmp_c, sel_c, fox_c = (_cache_t(c) for c in (cache_sb_kv, cache_cmp_kv, cache_sel_kv, cache_fox_kv))
    lf_c = jnp.transpose(cache_fox_logf, (0, 1, 3, 2))
    wb = state_win_kv.shape[2]
    win_c = jnp.transpose(state_win_kv, (0, 1, 3, 4, 5, 2)).reshape(depth, n_seq, KV_ROWS_NSA, wb)
    row = lambda a, l: a[l][None, :]
    outs = {k: [] for k in ("sb_p", "sb_s", "cmp_p", "cmp_s", "sel_p", "sel_s", "fox_p", "fox_s", "lf_p", "lf_s",
                            "win_p", "win_s")}
    ffn_tm_p = 512 if seq % 512 == 0 else ATT_TILE
    for l in range(depth):
        lw = _layer_weights(l, w_in, b_forget, cmp_w, w_br_sb, w_br_nsa, w_br_fox, w_out, ATT_TILE)
        lw_s = dict(lw, cw=lw["cw"][:tm_s])
        wgu1, wd1 = ffn1_w_gu[l].astype(BF16), ffn1_w_down[l].astype(BF16)
        wgu2, wd2 = ffn2_w_gu[l].astype(BF16), ffn2_w_down[l].astype(BF16)
        x_p = _ffn(x_p, row(ffn1_pre_g, l), row(ffn1_post_g, l), wgu1, wd1, ffn_tm_p)
        x_s = _ffn(x_s, row(ffn1_pre_g, l), row(ffn1_post_g, l), wgu1, wd1, tm_s)

        pp = _proj(x_p, row(mix_pre_g, l), lw, ATT_TILE)
        ckt = jnp.transpose(pp["ck"].reshape(seq // CMP_BLOCK, KV_ROWS_NSA))
        o_sb = _sb_prompt(pp["sbq"], pp["kvt3"], Q_TILE)
        o_fox = _fox_prompt(pp["foxq"], pp["kvt3"], pp["fcum"], pp["ft3"], Q_TILE)
        o_nsa = _nsa_prompt(pp["nsaq"], ckt, pp["kvt3"], pp["small"], Q_TILE)
        x_p = _merge(x_p, o_sb, o_nsa, o_fox, row(mix_pre_g, l), row(mix_post_g, l), lw, ATT_TILE)

        ps = _proj(x_s, row(mix_pre_g, l), lw_s, tm_s)
        kvt_s = ps["kvt"]
        new = lambda r0, n: _new_pages(kvt_s[r0:r0 + n], n_seq, n_new)
        lf_new = _new_pages(jnp.transpose(ps["small"][:, LOGF_LANE:LOGF_LANE + FOX_HEADS]), n_seq, n_new)
        o_sb = _sb_sample(ps["sbq"], new(ROW_SB, KV_ROWS_SB), sb_c, page_table, l, n_new)
        o_fox = _fox_sample(ps["foxq"], new(ROW_FOX, KV_ROWS_FOX), lf_new, fox_c, lf_c, page_table, l, n_new)
        o_cmp, sel = _cmp_sample(ps["nsaq"], lw["cw_row"], cmp_c, page_table, l, n_new, past_len)
        o_sel = _sel_sample(ps["nsaq"], new(ROW_SEL, KV_ROWS_NSA), sel, sel_c, page_table, l, n_new, past_len)
        new_win = new(ROW_WIN, KV_ROWS_NSA)
        o_nsa = _win_sample(ps["nsaq"], new_win, win_c[l], o_cmp, o_sel, ps["small"], n_new)
        x_s = _merge(x_s, o_sb, o_nsa, o_fox, row(mix_pre_g, l), row(mix_post_g, l), lw, tm_s)

        x_p = _ffn(x_p, row(ffn2_pre_g, l), row(ffn2_post_g, l), wgu2, wd2, ffn_tm_p)
        x_s = _ffn(x_s, row(ffn2_pre_g, l), row(ffn2_post_g, l), wgu2, wd2, tm_s)

        kvt_p = pp["kvt"]
        outs["sb_p"].append(_kv_out(kvt_p[ROW_SB:ROW_SB + 512], (batch, seq), SB_HEADS))
        outs["sb_s"].append(_kv_out(kvt_s[ROW_SB:ROW_SB + 512], (n_seq, n_new), SB_HEADS))
        outs["cmp_p"].append(_kv_out(kvt_p[ROW_CMP:ROW_CMP + 256], (batch, seq), NSA_KV_GROUPS))
        outs["cmp_s"].append(_kv_out(kvt_s[ROW_CMP:ROW_CMP + 256], (n_seq, n_new), NSA_KV_GROUPS))
        outs["sel_p"].append(_kv_out(kvt_p[ROW_SEL:ROW_SEL + 256], (batch, seq), NSA_KV_GROUPS))
        outs["sel_s"].append(_kv_out(kvt_s[ROW_SEL:ROW_SEL + 256], (n_seq, n_new), NSA_KV_GROUPS))
        outs["fox_p"].append(_kv_out(kvt_p[ROW_FOX:ROW_FOX + 512], (batch, seq), FOX_HEADS))
        outs["fox_s"].append(_kv_out(kvt_s[ROW_FOX:ROW_FOX + 512], (n_seq, n_new), FOX_HEADS))
        outs["lf_p"].append(pp["small"][:, LOGF_LANE:LOGF_LANE + FOX_HEADS].reshape(batch, seq, FOX_HEADS))
        outs["lf_s"].append(ps["small"][:, LOGF_LANE:LOGF_LANE + FOX_HEADS].reshape(n_seq, n_new, FOX_HEADS))
        w_keep = min(WINDOW, seq)
        outs["win_p"].append(_kv_out(kvt_p[ROW_WIN:ROW_WIN + 256, seq - w_keep:], (batch, w_keep), NSA_KV_GROUPS))
        win_all = jnp.concatenate([win_c[l], new_win[:, :, :n_new]], axis=2)[:, :, n_new:]
        outs["win_s"].append(jnp.transpose(win_all.reshape(n_seq, 2, NSA_KV_GROUPS, HEAD_DIM, wb), (0, 4, 1, 2, 3)))

    st = lambda k: jnp.stack(outs[k])
    return (x_p.reshape(batch, seq, D_MODEL), x_s.reshape(n_seq, n_new, D_MODEL),
            st("sb_p"), st("sb_s"), st("cmp_p"), st("cmp_s"), st("sel_p"), st("sel_s"),
            st("fox_p"), st("fox_s"), st("lf_p"), st("lf_s"), st("win_p"), st("win_s"))
```

```python
import functools

import jax
import jax.numpy as jnp
from jax import lax
from jax.experimental import pallas as pl
from jax.experimental.pallas import tpu as pltpu

F32 = jnp.float32
BF16 = jnp.bfloat16

D_MODEL = 1024
HEAD_DIM = 64
SB_HEADS = 4
NSA_HEADS = 8
NSA_KV_GROUPS = 2
NSA_GROUP_SIZE = NSA_HEADS // NSA_KV_GROUPS
FOX_HEADS = 4
CMP_BLOCK = 64
SEL_BLOCK = 64
SEL_TOPK = 16
WINDOW = 512
D_FF = 2816
PAGE_SIZE = 128
RMS_EPS = 1e-6
NEG_INF = -1e30
FORCE_SCORE = 1e4

LANES = 128
V7X_VMEM_LIMIT_BYTES = 56 * 1024 * 1024
ATT_TILE = 256
Q_TILE = 256
FOX_KEYS = 512
FFN_CHUNK = 1408
PAGES_PER_STEP = 8
SB_EXIT = 110.0

KV_ROWS_SB = 2 * SB_HEADS * HEAD_DIM
KV_ROWS_FOX = 2 * FOX_HEADS * HEAD_DIM
KV_ROWS_NSA = 2 * NSA_KV_GROUPS * HEAD_DIM
ROW_SB, ROW_FOX, ROW_CMP, ROW_SEL, ROW_WIN = 0, 512, 1024, 1280, 1536
KV_ROWS_ALL = 1792
QT_ROWS = 1280
KN_COLS = 512
EXP_ZERO = 110.0
GATE_LANE = 0
LOGF_LANE = 24

NT = (((1,), (1,)), ((), ()))


def _cparams(sem):
    return pltpu.CompilerParams(dimension_semantics=sem, vmem_limit_bytes=V7X_VMEM_LIMIT_BYTES)


def _rms(x, g):
    return x * lax.rsqrt(jnp.mean(x * x, axis=-1, keepdims=True) + RMS_EPS) * g


def _softplus(z):
    return jnp.maximum(z, 0.0) + jnp.log(1.0 + jnp.exp(-jnp.abs(z)))


def _sigmoid(z):
    return 1.0 / (1.0 + jnp.exp(-z))


def _split_bf16(x, n):
    parts = []
    r = x
    for _ in range(n):
        p = r.astype(BF16)
        parts.append(p)
        r = r - p.astype(F32)
    return parts


def _dot_split_lhs(x, u, n):
    return sum(jnp.dot(p, u, preferred_element_type=F32) for p in _split_bf16(x, n))


def _dot_split_rhs(u, x, n):
    return sum(jnp.dot(u, p, preferred_element_type=F32) for p in _split_bf16(x, n))


def _iota(shape, dim):
    return lax.broadcasted_iota(jnp.int32, shape, dim)


def _stack_heads(q, n_heads):
    lane_head = _iota(q.shape, 1) // HEAD_DIM
    return jnp.concatenate([jnp.where(lane_head == h, q, 0.0) for h in range(n_heads)], axis=0).astype(BF16)


def _stack_nsa(qpad):
    return jnp.concatenate([qpad[:, h * LANES:(h + 1) * LANES] for h in range(NSA_HEADS)], axis=0).astype(BF16)


def _diag_select(o_all, n_heads, tq):
    lane_head = _iota((tq, o_all.shape[1]), 1) // HEAD_DIM
    out = o_all[0:tq]
    for h in range(1, n_heads):
        out = jnp.where(lane_head == h, o_all[h * tq:(h + 1) * tq], out)
    return out


def _row_t(n_rows, tq):
    return _iota((n_rows, 1), 0) % tq


def _nsa_slopes(n_rows, tq):
    head = _iota((n_rows, 1), 0) // tq
    slope = jnp.full((n_rows, 1), 0.5, F32)
    for h in range(1, NSA_HEADS):
        slope = jnp.where(head == h, 2.0 ** -(h + 1), slope)
    return slope


def _flash_update(s, v_t, m_ref, l_ref, acc_ref):
    m_old = m_ref[...]
    m_new = jnp.maximum(m_old, jnp.max(s, axis=1, keepdims=True))
    alpha = jnp.exp(m_old - m_new)
    p = jnp.exp(s - m_new)
    l_ref[...] = alpha * l_ref[...] + jnp.sum(p, axis=1, keepdims=True)
    acc_ref[...] = alpha * acc_ref[...] + lax.dot_general(p.astype(BF16), v_t, NT, preferred_element_type=F32)
    m_ref[...] = m_new


def _flash_init(m_ref, l_ref, acc_ref):
    m_ref[...] = jnp.full(m_ref.shape, NEG_INF, F32)
    l_ref[...] = jnp.zeros(l_ref.shape, F32)
    acc_ref[...] = jnp.zeros(acc_ref.shape, F32)


def _flash_out(l_ref, acc_ref):
    return acc_ref[...] / jnp.maximum(l_ref[...], 1e-30)


def _flash_update_t(s, v_t, m_ref, l_ref, acc_ref):
    m_old = m_ref[...]
    m_new = jnp.maximum(m_old, jnp.max(s, axis=0, keepdims=True))
    alpha = jnp.exp(m_old - m_new)
    p = jnp.exp(s - m_new)
    l_ref[...] = alpha * l_ref[...] + jnp.sum(p, axis=0, keepdims=True)
    acc_ref[...] = alpha * acc_ref[...] + jnp.dot(v_t, p.astype(BF16), preferred_element_type=F32)
    m_ref[...] = m_new


def _top_blocks_t(score, n_pick):
    nb = score.shape[0]
    blk = _iota(score.shape, 0).astype(F32)
    sel = jnp.zeros(score.shape, F32)
    for _ in range(n_pick):
        m = jnp.max(score, axis=0, keepdims=True)
        first = jnp.min(jnp.where(score == m, blk, float(nb)), axis=0, keepdims=True)
        hit = blk == first
        sel = jnp.where(hit, 1.0, sel)
        score = jnp.where(hit, -jnp.inf, score)
    return sel


def _masked_softmax_rows(s, valid):
    s = jnp.where(valid, s, NEG_INF)
    m = jnp.max(s, axis=1, keepdims=True)
    p = jnp.where(valid, jnp.exp(s - m), 0.0)
    return p / jnp.maximum(jnp.sum(p, axis=1, keepdims=True), 1e-30)


def _top_blocks(score, n_pick):
    nb = score.shape[1]
    blk = _iota(score.shape, 1).astype(F32)
    sel = jnp.zeros(score.shape, F32)
    for _ in range(n_pick):
        m = jnp.max(score, axis=1, keepdims=True)
        first = jnp.min(jnp.where(score == m, blk, float(nb)), axis=1, keepdims=True)
        hit = blk == first
        sel = jnp.where(hit, 1.0, sel)
        score = jnp.where(hit, -jnp.inf, score)
    return sel


def _sb_step(qbd, k_t, v_t, u, valid, carry_ref, acc_ref):
    z = jnp.dot(qbd, k_t, preferred_element_type=F32)
    log_keep = -_softplus(z)
    if valid is not None:
        log_keep = jnp.where(valid, log_keep, 0.0)
    after = _dot_split_lhs(log_keep, u, 2)
    carry = carry_ref[...]
    w = jnp.exp(z + log_keep + after + carry)
    if valid is not None:
        w = jnp.where(valid, w, 0.0)
    acc_ref[...] += lax.dot_general(w.astype(BF16), v_t, NT, preferred_element_type=F32)
    carry_ref[...] = carry + after[:, 0:1] + log_keep[:, 0:1]


def _strict_upper(n):
    return jnp.where(_iota((n, n), 0) > _iota((n, n), 1), 1.0, 0.0).astype(BF16)


def _ffn_kernel(x_ref, pre_ref, post_ref, wgu_ref, wd_ref, o_ref):
    x = x_ref[...]
    h = _rms(x, pre_ref[...]).astype(BF16)
    acc = jnp.zeros(x.shape, F32)
    for c in range(D_FF // FFN_CHUNK):
        lo = c * FFN_CHUNK
        gate = jnp.dot(h, wgu_ref[:, lo:lo + FFN_CHUNK], preferred_element_type=F32)
        up = jnp.dot(h, wgu_ref[:, D_FF + lo:D_FF + lo + FFN_CHUNK], preferred_element_type=F32)
        a = (gate * _sigmoid(gate) * up).astype(BF16)
        acc = acc + jnp.dot(a, wd_ref[lo:lo + FFN_CHUNK, :], preferred_element_type=F32)
    o_ref[...] = x + 0.5 * _rms(acc, post_ref[...])


def _ffn(x, pre_g, post_g, w_gu, w_down, tm):
    m = x.shape[0]
    const = lambda i: (0, 0)
    return pl.pallas_call(
        _ffn_kernel,
        grid=(m // tm,),
        in_specs=[pl.BlockSpec((tm, D_MODEL), lambda i: (i, 0)),
                  pl.BlockSpec((1, D_MODEL), const), pl.BlockSpec((1, D_MODEL), const),
                  pl.BlockSpec((D_MODEL, 2 * D_FF), const, pipeline_mode=pl.Buffered(1)),
                  pl.BlockSpec((D_FF, D_MODEL), const, pipeline_mode=pl.Buffered(1))],
        out_specs=pl.BlockSpec((tm, D_MODEL), lambda i: (i, 0)),
        out_shape=jax.ShapeDtypeStruct((m, D_MODEL), F32),
        compiler_params=_cparams(("arbitrary",)),
        name="half_ffn",
    )(x, pre_g, post_g, w_gu, w_down)


def _proj_kernel(x_ref, g_ref, wq_ref, wkvt_ref, wqt_ref, wkn_ref, wcmp_ref, wsm_ref, brow_ref, cw_ref,
                 sbq_ref, nsaq_ref, foxq_ref, kvt_ref, kvt3_ref, qt_ref, kn_ref, ck_ref, small_ref, fcum_ref, gt3_ref,
                 kmax_ref, carry_ref, kmx_ref, *, tm):
    @pl.when(pl.program_id(0) == 0)
    def _():
        carry_ref[...] = jnp.zeros(carry_ref.shape, F32)
        kmx_ref[...] = jnp.zeros(kmx_ref.shape, F32)

    h = _rms(x_ref[...], g_ref[...]).astype(BF16)
    q = jnp.dot(h, wq_ref[...], preferred_element_type=F32)
    sbq_ref[...] = q[:, 0:256]
    nsaq_ref[...] = q[:, 256:1280]
    foxq_ref[...] = q[:, 1280:1536]
    kvt = lax.dot_general(wkvt_ref[...], h, NT, preferred_element_type=F32)
    kvt_ref[...] = kvt
    kvt3_ref[...] = kvt.astype(BF16)
    qt_ref[...] = lax.dot_general(wqt_ref[...], h, NT, preferred_element_type=F32).astype(BF16)
    kn = jnp.dot(h, wkn_ref[...], preferred_element_type=F32)
    kn_ref[...] = kn.astype(BF16)
    kf = kn[:, 0:256]
    head_of = jnp.where(_iota((256, LANES), 0) // HEAD_DIM == _iota((256, LANES), 1), 1.0, 0.0).astype(BF16)
    norm2 = jnp.max(jnp.dot((kf * kf).astype(BF16), head_of, preferred_element_type=F32), axis=0, keepdims=True)
    kmx_ref[...] = jnp.maximum(kmx_ref[...], norm2)
    kmax_ref[...] = kmx_ref[...]
    ycmp = jnp.dot(h, wcmp_ref[...], preferred_element_type=F32)
    ck_ref[...] = (ycmp * cw_ref[...]).reshape(tm // CMP_BLOCK, CMP_BLOCK, KV_ROWS_NSA).sum(axis=1)
    ys = jnp.dot(h, wsm_ref[...], preferred_element_type=F32)
    lane = _iota((1, LANES), 1)
    log_f = -_softplus(-(ys + brow_ref[...]))
    is_f = jnp.logical_and(lane >= LOGF_LANE, lane < LOGF_LANE + FOX_HEADS)
    small = jnp.where(lane < LOGF_LANE, _sigmoid(ys), jnp.where(is_f, log_f, 0.0))
    small_ref[...] = small
    gq = gt3_ref.shape[2]
    for j in range(tm // gq):
        gt3_ref[j] = small[j * gq:(j + 1) * gq].T
    tri = jnp.where(_iota((tm, tm), 0) >= _iota((tm, tm), 1), 1.0, 0.0).astype(BF16)
    cum = _dot_split_rhs(tri, jnp.where(is_f, log_f, 0.0), 3) + carry_ref[...]
    carry_ref[...] = cum[tm - 1:tm, :]
    fcum_ref[...] = cum


def _proj(x, g, lw, tm):
    m = x.shape[0]
    n = m // tm
    const = lambda i: (0, 0)
    row = lambda i: (i, 0)
    gq = min(Q_TILE, tm)
    outs = pl.pallas_call(
        functools.partial(_proj_kernel, tm=tm),
        grid=(n,),
        in_specs=[pl.BlockSpec((tm, D_MODEL), row), pl.BlockSpec((1, D_MODEL), const),
                  pl.BlockSpec((D_MODEL, 1536), const), pl.BlockSpec((KV_ROWS_ALL, D_MODEL), const),
                  pl.BlockSpec((QT_ROWS, D_MODEL), const), pl.BlockSpec((D_MODEL, KN_COLS), const),
                  pl.BlockSpec((D_MODEL, KV_ROWS_NSA), const), pl.BlockSpec((D_MODEL, LANES), const),
                  pl.BlockSpec((1, LANES), const), pl.BlockSpec((tm, KV_ROWS_NSA), const)],
        out_specs=[pl.BlockSpec((tm, 256), row), pl.BlockSpec((tm, 1024), row), pl.BlockSpec((tm, 256), row),
                   pl.BlockSpec((KV_ROWS_ALL, tm), lambda i: (0, i)),
                   pl.BlockSpec((None, KV_ROWS_ALL, tm), lambda i: (i, 0, 0)),
                   pl.BlockSpec((QT_ROWS, tm), lambda i: (0, i)),
                   pl.BlockSpec((tm, KN_COLS), row),
                   pl.BlockSpec((None, tm // CMP_BLOCK, KV_ROWS_NSA), lambda i: (i, 0, 0)),
                   pl.BlockSpec((tm, LANES), row), pl.BlockSpec((tm, LANES), row),
                   pl.BlockSpec((tm // gq, LANES, gq), lambda i: (i, 0, 0)),
                   pl.BlockSpec((1, LANES), const)],
        out_shape=[jax.ShapeDtypeStruct((m, 256), F32), jax.ShapeDtypeStruct((m, 1024), F32),
                   jax.ShapeDtypeStruct((m, 256), F32),
                   jax.ShapeDtypeStruct((KV_ROWS_ALL, m), F32),
                   jax.ShapeDtypeStruct((n, KV_ROWS_ALL, tm), BF16),
                   jax.ShapeDtypeStruct((QT_ROWS, m), BF16),
                   jax.ShapeDtypeStruct((m, KN_COLS), BF16),
                   jax.ShapeDtypeStruct((n, tm // CMP_BLOCK, KV_ROWS_NSA), F32),
                   jax.ShapeDtypeStruct((m, LANES), F32), jax.ShapeDtypeStruct((m, LANES), F32),
                   jax.ShapeDtypeStruct((m // gq, LANES, gq), F32),
                   jax.ShapeDtypeStruct((1, LANES), F32)],
        scratch_shapes=[pltpu.VMEM((1, LANES), F32), pltpu.VMEM((1, LANES), F32)],
        compiler_params=_cparams(("arbitrary",)),
        name="mixer_proj",
    )(x, g, lw["wq"], lw["wkvt"], lw["wqt"], lw["wkn"], lw["wcmp"], lw["wsm"], lw["brow"], lw["cw"])
    keys = ("sbq", "nsaq", "foxq", "kvt", "kvt3", "qt", "kn", "ck", "small", "fcum", "gt3", "kmax")
    return dict(zip(keys, outs))


def _merge_kernel(x_ref, osb_ref, onsa_ref, ofox_ref, pre_ref, post_ref, wg_ref, wsb_ref, wnsa_ref, wfox_ref,
                  wout_ref, o_ref):
    x = x_ref[...]
    h = _rms(x, pre_ref[...]).astype(BF16)
    mixed = jnp.zeros(x.shape, F32)
    for b, (o_b, w_b) in enumerate(((osb_ref, wsb_ref), (onsa_ref, wnsa_ref), (ofox_ref, wfox_ref))):
        gate = _sigmoid(jnp.dot(h, wg_ref[:, b * D_MODEL:(b + 1) * D_MODEL], preferred_element_type=F32))
        mixed = mixed + gate * jnp.dot(o_b[...].astype(BF16), w_b[...], preferred_element_type=F32)
    y = jnp.dot(mixed.astype(BF16), wout_ref[...], preferred_element_type=F32)
    o_ref[...] = x + _rms(y, post_ref[...])


def _merge(x, o_sb, o_nsa, o_fox, pre_g, post_g, lw, tm):
    m = x.shape[0]
    const = lambda i: (0, 0)
    row = lambda i: (i, 0)
    return pl.pallas_call(
        _merge_kernel,
        grid=(m // tm,),
        in_specs=[pl.BlockSpec((tm, D_MODEL), row), pl.BlockSpec((tm, 256), row), pl.BlockSpec((tm, 1024), row),
                  pl.BlockSpec((tm, 256), row), pl.BlockSpec((1, D_MODEL), const), pl.BlockSpec((1, D_MODEL), const),
                  pl.BlockSpec((D_MODEL, 3 * D_MODEL), const), pl.BlockSpec((256, D_MODEL), const),
                  pl.BlockSpec((1024, D_MODEL), const), pl.BlockSpec((256, D_MODEL), const),
                  pl.BlockSpec((D_MODEL, D_MODEL), const)],
        out_specs=pl.BlockSpec((tm, D_MODEL), row),
        out_shape=jax.ShapeDtypeStruct((m, D_MODEL), F32),
        compiler_params=_cparams(("arbitrary",)),
        name="mixer_merge",
    )(x, o_sb, o_nsa, o_fox, pre_g, post_g, lw["wgate"], lw["wbr_sb"], lw["wbr_nsa"], lw["wbr_fox"], lw["wout"])


def _sb_prompt_kernel(q_ref, kv_ref, o_ref, carry_ref, acc_ref, *, tq, tk):
    i = pl.program_id(0)
    n_rows = SB_HEADS * tq
    qbd = _stack_heads(q_ref[...], SB_HEADS)
    u = _strict_upper(tk)
    kb0 = (i * tq) // tk
    t_pos = i * tq + _row_t(n_rows, tq)
    carry_ref[...] = jnp.zeros(carry_ref.shape, F32)
    acc_ref[...] = jnp.zeros(acc_ref.shape, F32)

    def tile(kb, diagonal):
        k_t = kv_ref[kb, 0:256, :]
        v_t = kv_ref[kb, 256:512, :]
        valid = (kb * tk + _iota((1, tk), 1) < t_pos) if diagonal else None
        _sb_step(qbd, k_t, v_t, u, valid, carry_ref, acc_ref)

    tile(kb0, True)

    def cond(c):
        return jnp.logical_and(c[0] >= 0, c[1] > -SB_EXIT)

    def body(c):
        tile(c[0], False)
        return c[0] - 1, jnp.max(carry_ref[...])

    lax.while_loop(cond, body, (kb0 - 1, jnp.max(carry_ref[...])))
    o_ref[...] = _diag_select(acc_ref[...], SB_HEADS, tq).astype(o_ref.dtype)


def _sb_prompt(sbq, kvt3, tq):
    t = sbq.shape[0]
    n_kt, _, tk = kvt3.shape
    return pl.pallas_call(
        functools.partial(_sb_prompt_kernel, tq=tq, tk=tk),
        grid=(t // tq,),
        in_specs=[pl.BlockSpec((tq, 256), lambda i: (i, 0)),
                  pl.BlockSpec((n_kt, KV_ROWS_SB, tk), lambda i: (0, ROW_SB // KV_ROWS_SB, 0))],
        out_specs=pl.BlockSpec((tq, 256), lambda i: (i, 0)),
        out_shape=jax.ShapeDtypeStruct((t, 256), BF16),
        scratch_shapes=[pltpu.VMEM((SB_HEADS * tq, 1), F32), pltpu.VMEM((SB_HEADS * tq, 256), F32)],
        compiler_params=_cparams(("arbitrary",)),
        name="sb_prompt",
    )(sbq, kvt3)


def _lanes_per_head(row, n_heads, tq, lane0):
    return jnp.concatenate([jnp.broadcast_to(row[:, lane0 + h:lane0 + h + 1], (1, tq)) for h in range(n_heads)], axis=1)


def _fox_prompt_kernel(qt_ref, k_ref, vt_ref, fcum_ref, kmax_ref, o_ref, m_ref, l_ref, acc_ref, *, tq, tk):
    i = pl.program_id(0)
    n_lanes = FOX_HEADS * tq
    qt = qt_ref[...]
    row_head = _iota(qt.shape, 0) // HEAD_DIM
    qbd_t = jnp.concatenate([jnp.where(row_head == h, qt, jnp.zeros_like(qt)) for h in range(FOX_HEADS)], axis=1)
    kb0 = (i * tq) // tk
    t_pos = i * tq + _iota((1, n_lanes), 1) % tq
    q32 = qbd_t.astype(F32)
    q_norm = jnp.sqrt(jnp.sum(q32 * q32, axis=0, keepdims=True))
    z_bound = 1.02 * q_norm * jnp.sqrt(_lanes_per_head(kmax_ref[...], FOX_HEADS, tq, 0))
    _flash_init(m_ref, l_ref, acc_ref)

    v_tiles = tk // vt_ref.shape[2]

    def tile(kb, diagonal):
        start = pl.multiple_of(kb * tk, tk)
        f = fcum_ref[pl.ds(start, tk), :]
        fk = jnp.concatenate([jnp.broadcast_to(f[:, LOGF_LANE + h:LOGF_LANE + h + 1], (tk, tq))
                              for h in range(FOX_HEADS)], axis=1)
        s = jnp.dot(k_ref[pl.ds(start, tk), :], qbd_t, preferred_element_type=F32) - fk
        if diagonal:
            s = jnp.where(kb * tk + _iota((tk, n_lanes), 0) <= t_pos, s, NEG_INF)
        v_t = jnp.concatenate([vt_ref[kb * v_tiles + j] for j in range(v_tiles)], axis=1)
        _flash_update_t(s, v_t, m_ref, l_ref, acc_ref)

    def live(kb):
        f_end = fcum_ref[pl.ds(kb * tk + tk - 1, 1), :]
        bound = z_bound - _lanes_per_head(f_end, FOX_HEADS, tq, LOGF_LANE)
        return jnp.max(bound - m_ref[...]) > -EXP_ZERO

    tile(kb0, True)

    def cond(c):
        return c[1]

    def body(c):
        tile(c[0], False)
        nxt = c[0] - 1
        return nxt, jnp.logical_and(nxt >= 0, live(jnp.maximum(nxt, 0)))

    first = kb0 - 1
    lax.while_loop(cond, body, (first, jnp.logical_and(first >= 0, live(jnp.maximum(first, 0)))))
    o_t = _flash_out(l_ref, acc_ref)
    o_t = jnp.concatenate([o_t[h * HEAD_DIM:(h + 1) * HEAD_DIM, h * tq:(h + 1) * tq] for h in range(FOX_HEADS)], axis=0)
    o_ref[...] = o_t.T.astype(o_ref.dtype)


def _resident(shape, index):
    return pl.BlockSpec(shape, index, pipeline_mode=pl.Buffered(1))


def _fox_prompt(qt, kn, kvt3, fcum, kmax, tq, tk):
    t = kn.shape[0]
    n_vt, _, tv = kvt3.shape
    n_lanes = FOX_HEADS * tq
    assert tk % tv == 0 and tk % tq == 0 and t % tk == 0
    return pl.pallas_call(
        functools.partial(_fox_prompt_kernel, tq=tq, tk=tk),
        grid=(t // tq,),
        in_specs=[pl.BlockSpec((256, tq), lambda i: (4, i)),
                  _resident((t, 256), lambda i: (0, 0)),
                  _resident((n_vt, 256, tv), lambda i: (0, (ROW_FOX + 256) // 256, 0)),
                  _resident((t, LANES), lambda i: (0, 0)),
                  pl.BlockSpec((1, LANES), lambda i: (0, 0))],
        out_specs=pl.BlockSpec((tq, 256), lambda i: (i, 0)),
        out_shape=jax.ShapeDtypeStruct((t, 256), BF16),
        scratch_shapes=[pltpu.VMEM((1, n_lanes), F32), pltpu.VMEM((1, n_lanes), F32), pltpu.VMEM((256, n_lanes), F32)],
        compiler_params=_cparams(("arbitrary",)),
        name="fox_prompt",
    )(qt, kn, kvt3, fcum, kmax)


def _group_rows(x_g0, x_g1):
    return jnp.concatenate([x_g0] * NSA_GROUP_SIZE + [x_g1] * NSA_GROUP_SIZE, axis=0)


def _slope_lanes(heads, tq):
    return jnp.concatenate([jnp.full((1, tq), 2.0 ** -(h + 1), F32) for h in heads], axis=1)


def _stack_nsa_t(qt_ref, heads):
    return jnp.concatenate([qt_ref[h * LANES:(h + 1) * LANES, :] for h in heads], axis=1)


def _nsa_select_kernel(qt_ref, ck_ref, ckt_ref, ocmp_ref, sel_ref, flag_ref, *, tq, tk, n_blocks):
    i = pl.program_id(0)
    n_lanes = NSA_HEADS * tq
    qbd_t = _stack_nsa_t(qt_ref, range(NSA_HEADS))
    slope = _slope_lanes(range(NSA_HEADS), tq)
    t_pos = i * tq + _iota((1, n_lanes), 1) % tq
    cmp_end = _iota((n_blocks, n_lanes), 0) * CMP_BLOCK + (CMP_BLOCK - 1)
    s_c = jnp.dot(ck_ref[:, 0:128].astype(BF16), qbd_t, preferred_element_type=F32)
    s_c = s_c - slope * (t_pos - cmp_end).astype(F32)
    valid = cmp_end <= t_pos
    s_c = jnp.where(valid, s_c, NEG_INF)
    p = jnp.where(valid, jnp.exp(s_c - jnp.max(s_c, axis=0, keepdims=True)), 0.0)
    p_c = p / jnp.maximum(jnp.sum(p, axis=0, keepdims=True), 1e-30)
    ocmp_ref[...] = jnp.dot(ckt_ref[128:256, :].astype(BF16), p_c.astype(BF16), preferred_element_type=F32)

    cur = (i * tq + _iota((1, tq), 1)) // SEL_BLOCK
    blk = _iota((n_blocks, tq), 0)
    forced = jnp.logical_or(jnp.logical_or(blk == cur, blk == cur - 1), blk == 0)
    n_kt = flag_ref.shape[1]
    tile_of = jnp.where(_iota((n_kt, n_blocks), 1) // (tk // SEL_BLOCK) == _iota((n_kt, n_blocks), 0), 1.0, 0.0).astype(BF16)
    for g in range(NSA_KV_GROUPS):
        score = sum(p_c[:, (g * NSA_GROUP_SIZE + r) * tq:(g * NSA_GROUP_SIZE + r + 1) * tq] for r in range(NSA_GROUP_SIZE))
        score = jnp.where(forced, FORCE_SCORE, jnp.where(blk > cur, -FORCE_SCORE, score))
        sel = _top_blocks_t(score, min(SEL_TOPK, n_blocks))
        sel_ref[g * n_blocks:(g + 1) * n_blocks, :] = jnp.where(sel > 0.5, 0.0, NEG_INF)
        hits = jnp.dot(tile_of, sel.astype(BF16), preferred_element_type=F32)
        flag_ref[g] = (jnp.max(hits, axis=1, keepdims=True) > 0.5).astype(jnp.int32)


def _nsa_select(qt, ck, ckt, tq, tk):
    t = qt.shape[1]
    n_blocks = ck.shape[0]
    n_qt, n_kt = t // tq, t // tk
    return pl.pallas_call(
        functools.partial(_nsa_select_kernel, tq=tq, tk=tk, n_blocks=n_blocks),
        grid=(n_qt,),
        in_specs=[pl.BlockSpec((NSA_HEADS * LANES, tq), lambda i: (0, i)),
                  pl.BlockSpec((n_blocks, KV_ROWS_NSA), lambda i: (0, 0)),
                  pl.BlockSpec((KV_ROWS_NSA, n_blocks), lambda i: (0, 0))],
        out_specs=[pl.BlockSpec((None, LANES, NSA_HEADS * tq), lambda i: (i, 0, 0)),
                   pl.BlockSpec((None, NSA_KV_GROUPS * n_blocks, tq), lambda i: (i, 0, 0)),
                   pl.BlockSpec((None, NSA_KV_GROUPS, n_kt, 1), lambda i: (i, 0, 0, 0))],
        out_shape=[jax.ShapeDtypeStruct((n_qt, LANES, NSA_HEADS * tq), F32),
                   jax.ShapeDtypeStruct((n_qt, NSA_KV_GROUPS * n_blocks, tq), F32),
                   jax.ShapeDtypeStruct((n_qt, NSA_KV_GROUPS, n_kt, 1), jnp.int32)],
        compiler_params=_cparams(("arbitrary",)),
        name="nsa_select",
    )(qt, ck, ckt)


def _nsa_attend_kernel(flag_ref, qt_ref, ksel_ref, vsel_ref, kwin_ref, vwin_ref, sel_ref, ocmp_ref, gate_ref,
                       o_ref, m_ref, l_ref, acc_ref, *, tq, tk, n_blocks):
    i = pl.program_id(0)
    n_kt = vsel_ref.shape[0]
    n_lanes = NSA_GROUP_SIZE * tq
    kb0 = (i * tq) // tk
    blocks_per_tile = tk // SEL_BLOCK
    tiles_per_chunk = 8 // blocks_per_tile
    assert blocks_per_tile * tiles_per_chunk == 8 and n_blocks % 8 == 0
    rel = (_iota((1, n_lanes), 1) % tq - _iota((tk, n_lanes), 0)).astype(F32)
    gates = gate_ref[...]
    for g in range(NSA_KV_GROUPS):
        heads = range(g * NSA_GROUP_SIZE, (g + 1) * NSA_GROUP_SIZE)
        qbd_t = _stack_nsa_t(qt_ref, heads)
        slope = _slope_lanes(heads, tq)
        bias_rel = slope * rel

        def tile(kb, k_ref, vt_ref, selected, diagonal):
            start = pl.multiple_of(kb * tk, tk)
            off = (i * tq - kb * tk).astype(F32)
            s = jnp.dot(k_ref[pl.ds(start, tk), :], qbd_t, preferred_element_type=F32) - bias_rel - slope * off
            if selected:
                chunk = sel_ref[pl.ds(pl.multiple_of(g * n_blocks + (kb // tiles_per_chunk) * 8, 8), 8), :]
                sub = kb % tiles_per_chunk
                rows = chunk[0:blocks_per_tile]
                for c in range(1, tiles_per_chunk):
                    rows = jnp.where(sub == c, chunk[c * blocks_per_tile:(c + 1) * blocks_per_tile], rows)
                mask = jnp.concatenate([jnp.broadcast_to(rows[b:b + 1, :], (SEL_BLOCK, tq))
                                        for b in range(blocks_per_tile)], axis=0)
                s = s + jnp.concatenate([mask] * NSA_GROUP_SIZE, axis=1)
                if diagonal:
                    s = jnp.where(rel >= -off, s, NEG_INF)
            else:
                dist = rel + off
                s = jnp.where(dist >= 0.0, jnp.where(dist < float(WINDOW), s, NEG_INF), NEG_INF)
            _flash_update_t(s, vt_ref[kb], m_ref, l_ref, acc_ref)

        _flash_init(m_ref, l_ref, acc_ref)

        def sel_body(kb, c):
            @pl.when(flag_ref[(i * NSA_KV_GROUPS + g) * n_kt + kb] > 0)
            def _():
                tile(kb, ksel_ref, vsel_ref, True, False)
            return c

        lax.fori_loop(0, kb0, sel_body, 0)
        tile(kb0, ksel_ref, vsel_ref, True, True)
        o_sel = _flash_out(l_ref, acc_ref)

        _flash_init(m_ref, l_ref, acc_ref)

        def win_body(kb, c):
            tile(kb, kwin_ref, vwin_ref, False, False)
            return c

        lax.fori_loop(jnp.maximum(i * tq - (WINDOW - 1), 0) // tk, kb0 + 1, win_body, 0)
        o_win = _flash_out(l_ref, acc_ref)

        for r, h in enumerate(heads):
            lanes = slice(r * tq, (r + 1) * tq)
            o_h = (gates[GATE_LANE + h:GATE_LANE + h + 1, :] * ocmp_ref[:, h * tq:(h + 1) * tq]
                   + gates[GATE_LANE + NSA_HEADS + h:GATE_LANE + NSA_HEADS + h + 1, :] * o_sel[:, lanes]
                   + gates[GATE_LANE + 2 * NSA_HEADS + h:GATE_LANE + 2 * NSA_HEADS + h + 1, :] * o_win[:, lanes])
            o_ref[:, h * LANES:(h + 1) * LANES] = o_h.T.astype(o_ref.dtype)


def _nsa_attend(flags, qt, kn, kvt3, sel, ocmp, gt3, tq):
    t = kn.shape[0]
    n_kt, _, tk = kvt3.shape
    n_blocks = sel.shape[1] // NSA_KV_GROUPS
    n_lanes = NSA_GROUP_SIZE * tq
    grid_spec = pltpu.PrefetchScalarGridSpec(
        num_scalar_prefetch=1, grid=(t // tq,),
        in_specs=[pl.BlockSpec((NSA_HEADS * LANES, tq), lambda i, f: (0, i)),
                  _resident((t, LANES), lambda i, f: (0, 2)),
                  _resident((n_kt, LANES, tk), lambda i, f: (0, (ROW_SEL + LANES) // LANES, 0)),
                  _resident((t, LANES), lambda i, f: (0, 3)),
                  _resident((n_kt, LANES, tk), lambda i, f: (0, (ROW_WIN + LANES) // LANES, 0)),
                  pl.BlockSpec((None, NSA_KV_GROUPS * n_blocks, tq), lambda i, f: (i, 0, 0)),
                  pl.BlockSpec((None, LANES, NSA_HEADS * tq), lambda i, f: (i, 0, 0)),
                  pl.BlockSpec((None, LANES, tq), lambda i, f: (i, 0, 0))],
        out_specs=pl.BlockSpec((tq, NSA_HEADS * LANES), lambda i, f: (i, 0)),
        scratch_shapes=[pltpu.VMEM((1, n_lanes), F32), pltpu.VMEM((1, n_lanes), F32), pltpu.VMEM((LANES, n_lanes), F32)])
    return pl.pallas_call(
        functools.partial(_nsa_attend_kernel, tq=tq, tk=tk, n_blocks=n_blocks), grid_spec=grid_spec,
        out_shape=jax.ShapeDtypeStruct((t, NSA_HEADS * LANES), BF16),
        compiler_params=_cparams(("arbitrary",)), name="nsa_attend",
    )(flags, qt, kn, kvt3, kn, kvt3, sel, ocmp, gt3)


def _page_specs(rows, n_pages, layer, reverse):
    def spec(j):
        def index(b, s, pt):
            slot = s * PAGES_PER_STEP + j
            if reverse:
                slot = n_pages - 1 - slot
            return (layer, pt[b, slot], 0, 0)
        return pl.BlockSpec((None, None, rows, PAGE_SIZE), index)
    return [spec(j) for j in range(PAGES_PER_STEP)]


def _page_suffix_sums(x, u, n_split):
    r = x.shape[0]
    n = x.shape[1] // PAGE_SIZE
    stacked = jnp.concatenate([x[:, j * PAGE_SIZE:(j + 1) * PAGE_SIZE] for j in range(n)], axis=0)
    y = _dot_split_lhs(stacked, u, n_split)
    tot = y[:, 0:1] + stacked[:, 0:1]
    inner = jnp.concatenate([y[j * r:(j + 1) * r] for j in range(n)], axis=1)
    return inner, [tot[j * r:(j + 1) * r] for j in range(n)]


def _page_offsets(carry, totals):
    r = carry.shape[0]
    cols = []
    for tot in totals:
        cols.append(jnp.broadcast_to(carry, (r, PAGE_SIZE)))
        carry = carry + tot
    return jnp.concatenate(cols, axis=1), carry


def _sb_sample_kernel(pt_ref, q_ref, new_ref, *rest, n_new):
    pages = rest[:PAGES_PER_STEP]
    o_ref, carry_ref, acc_ref = rest[PAGES_PER_STEP:]
    s = pl.program_id(1)
    n_rows = SB_HEADS * n_new
    qbd = _stack_heads(q_ref[...], SB_HEADS)
    u = _strict_upper(PAGE_SIZE)

    @pl.when(s == 0)
    def _():
        carry_ref[...] = jnp.zeros(carry_ref.shape, F32)
        acc_ref[...] = jnp.zeros(acc_ref.shape, F32)
        new = new_ref[...].astype(BF16)
        valid = _iota((1, PAGE_SIZE), 1) < _row_t(n_rows, n_new)
        _sb_step(qbd, new[0:256], new[256:512], u, valid, carry_ref, acc_ref)

    @pl.when(jnp.max(carry_ref[...]) > -SB_EXIT)
    def _():
        kv = jnp.concatenate([page[...] for page in pages], axis=1).astype(BF16)
        z = jnp.dot(qbd, kv[0:256], preferred_element_type=F32)
        log_keep = -_softplus(z)
        inner, totals = _page_suffix_sums(log_keep, u, 2)
        offsets, carry = _page_offsets(carry_ref[...], totals)
        w = jnp.exp(z + log_keep + inner + offsets)
        acc_ref[...] += lax.dot_general(w.astype(BF16), kv[256:512], NT, preferred_element_type=F32)
        carry_ref[...] = carry

    @pl.when(s == pl.num_programs(1) - 1)
    def _():
        o_ref[...] = _diag_select(acc_ref[...], SB_HEADS, n_new)


def _sb_sample(q, new_t, cache_t, page_table, layer, n_new):
    n_seq, n_pages = page_table.shape
    n_rows = SB_HEADS * n_new
    grid_spec = pltpu.PrefetchScalarGridSpec(
        num_scalar_prefetch=1, grid=(n_seq, n_pages // PAGES_PER_STEP),
        in_specs=[pl.BlockSpec((n_new, 256), lambda b, s, pt: (b, 0)),
                  pl.BlockSpec((None, KV_ROWS_SB, PAGE_SIZE), lambda b, s, pt: (b, 0, 0))]
        + _page_specs(KV_ROWS_SB, n_pages, layer, True),
        out_specs=pl.BlockSpec((n_new, 256), lambda b, s, pt: (b, 0)),
        scratch_shapes=[pltpu.VMEM((n_rows, 1), F32), pltpu.VMEM((n_rows, 256), F32)])
    return pl.pallas_call(
        functools.partial(_sb_sample_kernel, n_new=n_new), grid_spec=grid_spec,
        out_shape=jax.ShapeDtypeStruct((n_seq * n_new, 256), F32),
        compiler_params=_cparams(("arbitrary", "arbitrary")), name="sb_sample",
    )(page_table, q, new_t, *([cache_t] * PAGES_PER_STEP))


def _expand_heads(lf, n_new):
    n = lf.shape[1]
    return jnp.concatenate([jnp.broadcast_to(lf[h:h + 1, :], (n_new, n)) for h in range(FOX_HEADS)], axis=0)


def _fox_sample_kernel(pt_ref, q_ref, new_ref, lfnew_ref, *rest, n_new):
    pages = rest[:PAGES_PER_STEP]
    lf_pages = rest[PAGES_PER_STEP:2 * PAGES_PER_STEP]
    o_ref, m_ref, l_ref, acc_ref, sfx_ref, cnew_ref = rest[2 * PAGES_PER_STEP:]
    s = pl.program_id(1)
    n_rows = FOX_HEADS * n_new
    qbd = _stack_heads(q_ref[...], FOX_HEADS)
    u = _strict_upper(PAGE_SIZE)
    lane = _iota((1, PAGE_SIZE), 1)
    t_row = _row_t(n_rows, n_new)

    @pl.when(s == 0)
    def _():
        _flash_init(m_ref, l_ref, acc_ref)
        sfx_ref[...] = jnp.zeros(sfx_ref.shape, F32)
        lf = _expand_heads(lfnew_ref[...], n_new)
        incl = jnp.where(_iota((PAGE_SIZE, PAGE_SIZE), 0) <= _iota((PAGE_SIZE, PAGE_SIZE), 1), 1.0, 0.0).astype(BF16)
        cum = _dot_split_lhs(lf, incl, 3)
        c_t = jnp.sum(jnp.where(lane == t_row, cum, 0.0), axis=1, keepdims=True)
        cnew_ref[...] = c_t
        new = new_ref[...].astype(BF16)
        sc = jnp.dot(qbd, new[0:256], preferred_element_type=F32) + (c_t - cum)
        sc = jnp.where(lane <= t_row, sc, NEG_INF)
        _flash_update(sc, new[256:512], m_ref, l_ref, acc_ref)

    kv = jnp.concatenate([page[...] for page in pages], axis=1).astype(BF16)
    lf = _expand_heads(jnp.concatenate([lf_page[...] for lf_page in lf_pages], axis=1), n_new)
    inner, totals = _page_suffix_sums(lf, u, 3)
    offsets, sfx = _page_offsets(sfx_ref[...], totals)
    sc = jnp.dot(qbd, kv[0:256], preferred_element_type=F32) + (inner + offsets + cnew_ref[...])
    _flash_update(sc, kv[256:512], m_ref, l_ref, acc_ref)
    sfx_ref[...] = sfx

    @pl.when(s == pl.num_programs(1) - 1)
    def _():
        o_ref[...] = _diag_select(_flash_out(l_ref, acc_ref), FOX_HEADS, n_new)


def _fox_sample(q, new_t, lf_new, cache_t, lf_cache_t, page_table, layer, n_new):
    n_seq, n_pages = page_table.shape
    n_rows = FOX_HEADS * n_new
    grid_spec = pltpu.PrefetchScalarGridSpec(
        num_scalar_prefetch=1, grid=(n_seq, n_pages // PAGES_PER_STEP),
        in_specs=[pl.BlockSpec((n_new, 256), lambda b, s, pt: (b, 0)),
                  pl.BlockSpec((None, KV_ROWS_FOX, PAGE_SIZE), lambda b, s, pt: (b, 0, 0)),
                  pl.BlockSpec((None, FOX_HEADS, PAGE_SIZE), lambda b, s, pt: (b, 0, 0))]
        + _page_specs(KV_ROWS_FOX, n_pages, layer, True) + _page_specs(FOX_HEADS, n_pages, layer, True),
        out_specs=pl.BlockSpec((n_new, 256), lambda b, s, pt: (b, 0)),
        scratch_shapes=[pltpu.VMEM((n_rows, 1), F32), pltpu.VMEM((n_rows, 1), F32), pltpu.VMEM((n_rows, 256), F32),
                        pltpu.VMEM((n_rows, 1), F32), pltpu.VMEM((n_rows, 1), F32)])
    return pl.pallas_call(
        functools.partial(_fox_sample_kernel, n_new=n_new), grid_spec=grid_spec,
        out_shape=jax.ShapeDtypeStruct((n_seq * n_new, 256), F32),
        compiler_params=_cparams(("arbitrary", "arbitrary")), name="fox_sample",
    )(page_table, q, new_t, lf_new, *([cache_t] * PAGES_PER_STEP), *([lf_cache_t] * PAGES_PER_STEP))


def _write_head_padded(o_all, n_new, o_ref):
    for h in range(NSA_HEADS):
        o_ref[:, h * LANES:(h + 1) * LANES] = o_all[h * n_new:(h + 1) * n_new]


def _cmp_sample_kernel(pt_ref, q_ref, cw_ref, *rest, n_new, past_len):
    pages = rest[:PAGES_PER_STEP]
    o_ref, sel_ref, ckt_ref = rest[PAGES_PER_STEP:]
    s = pl.program_id(1)
    n_rows = NSA_HEADS * n_new
    n_chunks, _, chunk_lanes = ckt_ref.shape
    n_blocks = n_chunks * chunk_lanes
    n_keys = PAGES_PER_STEP * PAGE_SIZE
    blocks_per_step = n_keys // CMP_BLOCK
    steps_per_chunk = chunk_lanes // blocks_per_step
    blk_lane = _iota((1, n_blocks), 1)

    @pl.when(s == 0)
    def _():
        ckt_ref[...] = jnp.zeros(ckt_ref.shape, F32)

    keys = jnp.concatenate([page[...] for page in pages], axis=1).astype(BF16)
    col = (s % steps_per_chunk) * blocks_per_step + _iota((n_keys, chunk_lanes), 0) // CMP_BLOCK
    placed = jnp.where(_iota((n_keys, chunk_lanes), 1) == col, cw_ref[...], 0.0).astype(BF16)
    ckt_ref[s // steps_per_chunk] += jnp.dot(keys, placed, preferred_element_type=F32)

    @pl.when(s == pl.num_programs(1) - 1)
    def _():
        qbd = _stack_nsa(q_ref[...])
        slope = _nsa_slopes(n_rows, n_new)
        t_pos = past_len + _row_t(n_rows, n_new)
        cmp_end = blk_lane * CMP_BLOCK + (CMP_BLOCK - 1)
        ck = jnp.concatenate([ckt_ref[c] for c in range(n_chunks)], axis=1)
        s_c = jnp.dot(qbd, ck[0:128].astype(BF16), preferred_element_type=F32)
        s_c = s_c - slope * (t_pos - cmp_end).astype(F32)
        p_c = _masked_softmax_rows(s_c, cmp_end <= t_pos)
        o_cmp = lax.dot_general(p_c.astype(BF16), ck[128:256].astype(BF16), NT, preferred_element_type=F32)
        _write_head_padded(o_cmp, n_new, o_ref)
        cur = (past_len + _iota((n_new, 1), 0)) // SEL_BLOCK
        blk = _iota((n_new, n_blocks), 1)
        forced = jnp.logical_or(blk == cur - 1, blk == 0)
        for g in range(NSA_KV_GROUPS):
            score = sum(p_c[(g * NSA_GROUP_SIZE + r) * n_new:(g * NSA_GROUP_SIZE + r + 1) * n_new]
                        for r in range(NSA_GROUP_SIZE))
            score = jnp.where(forced, FORCE_SCORE, jnp.where(blk > cur, -FORCE_SCORE, score))
            sel_ref[g * n_new:(g + 1) * n_new, :] = _top_blocks(score, min(SEL_TOPK - 1, n_blocks))


def _cmp_sample(nsaq, cw_keys, cache_t, page_table, layer, n_new, past_len):
    n_seq, n_pages = page_table.shape
    n_blocks = past_len // CMP_BLOCK
    n_keys, chunk_lanes = cw_keys.shape
    assert n_blocks % chunk_lanes == 0 and chunk_lanes % (n_keys // CMP_BLOCK) == 0
    grid_spec = pltpu.PrefetchScalarGridSpec(
        num_scalar_prefetch=1, grid=(n_seq, n_pages // PAGES_PER_STEP),
        in_specs=[pl.BlockSpec((n_new, 1024), lambda b, s, pt: (b, 0)),
                  pl.BlockSpec((n_keys, chunk_lanes), lambda b, s, pt: (0, 0))]
        + _page_specs(KV_ROWS_NSA, n_pages, layer, False),
        out_specs=[pl.BlockSpec((n_new, 1024), lambda b, s, pt: (b, 0)),
                   pl.BlockSpec((None, NSA_KV_GROUPS * n_new, n_blocks), lambda b, s, pt: (b, 0, 0))],
        scratch_shapes=[pltpu.VMEM((n_blocks // chunk_lanes, KV_ROWS_NSA, chunk_lanes), F32)])
    return pl.pallas_call(
        functools.partial(_cmp_sample_kernel, n_new=n_new, past_len=past_len), grid_spec=grid_spec,
        out_shape=[jax.ShapeDtypeStruct((n_seq * n_new, 1024), F32),
                   jax.ShapeDtypeStruct((n_seq, NSA_KV_GROUPS * n_new, n_blocks), F32)],
        compiler_params=_cparams(("arbitrary", "arbitrary")), name="cmp_sample",
    )(page_table, nsaq, cw_keys, *([cache_t] * PAGES_PER_STEP))


def _sel_sample_kernel(pt_ref, q_ref, new_ref, sel_ref, *rest, n_new, past_len):
    pages = rest[:PAGES_PER_STEP]
    o_ref, m_ref, l_ref, acc_ref = rest[PAGES_PER_STEP:]
    s = pl.program_id(1)
    n_rows = NSA_HEADS * n_new
    n_blocks = sel_ref.shape[1]
    qbd = _stack_nsa(q_ref[...])
    slope = _nsa_slopes(n_rows, n_new)
    t_row = _row_t(n_rows, n_new)
    lane = _iota((1, PAGE_SIZE), 1)
    blocks_per_page = PAGE_SIZE // SEL_BLOCK

    @pl.when(s == 0)
    def _():
        _flash_init(m_ref, l_ref, acc_ref)
        new = new_ref[...].astype(BF16)
        dist = (t_row - lane).astype(F32)
        sc = jnp.dot(qbd, new[0:128], preferred_element_type=F32) - slope * dist
        _flash_update(jnp.where(dist >= 0.0, sc, NEG_INF), new[128:256], m_ref, l_ref, acc_ref)

    sel = sel_ref[...].astype(BF16)
    n_keys = PAGES_PER_STEP * PAGE_SIZE
    kv = jnp.concatenate([page[...] for page in pages], axis=1).astype(BF16)
    expand = jnp.where(_iota((n_blocks, n_keys), 0)
                       == s * (PAGES_PER_STEP * blocks_per_page) + _iota((n_blocks, n_keys), 1) // SEL_BLOCK,
                       1.0, 0.0).astype(BF16)
    ch = jnp.dot(sel, expand, preferred_element_type=F32)
    chosen = _group_rows(ch[0:n_new], ch[n_new:2 * n_new])
    dist = (past_len + t_row - (s * n_keys + _iota((1, n_keys), 1))).astype(F32)
    sc = jnp.dot(qbd, kv[0:128], preferred_element_type=F32) - slope * dist
    _flash_update(jnp.where(chosen > 0.5, sc, NEG_INF), kv[128:256], m_ref, l_ref, acc_ref)

    @pl.when(s == pl.num_programs(1) - 1)
    def _():
        _write_head_padded(_flash_out(l_ref, acc_ref), n_new, o_ref)


def _sel_sample(nsaq, new_t, sel, cache_t, page_table, layer, n_new, past_len):
    n_seq, n_pages = page_table.shape
    n_rows = NSA_HEADS * n_new
    n_blocks = sel.shape[2]
    grid_spec = pltpu.PrefetchScalarGridSpec(
        num_scalar_prefetch=1, grid=(n_seq, n_pages // PAGES_PER_STEP),
        in_specs=[pl.BlockSpec((n_new, 1024), lambda b, s, pt: (b, 0)),
                  pl.BlockSpec((None, KV_ROWS_NSA, PAGE_SIZE), lambda b, s, pt: (b, 0, 0)),
                  pl.BlockSpec((None, NSA_KV_GROUPS * n_new, n_blocks), lambda b, s, pt: (b, 0, 0))]
        + _page_specs(KV_ROWS_NSA, n_pages, layer, False),
        out_specs=pl.BlockSpec((n_new, 1024), lambda b, s, pt: (b, 0)),
        scratch_shapes=[pltpu.VMEM((n_rows, 1), F32), pltpu.VMEM((n_rows, 1), F32), pltpu.VMEM((n_rows, LANES), F32)])
    return pl.pallas_call(
        functools.partial(_sel_sample_kernel, n_new=n_new, past_len=past_len), grid_spec=grid_spec,
        out_shape=jax.ShapeDtypeStruct((n_seq * n_new, 1024), F32),
        compiler_params=_cparams(("arbitrary", "arbitrary")), name="sel_sample",
    )(page_table, nsaq, new_t, sel, *([cache_t] * PAGES_PER_STEP))


def _win_sample_kernel(q_ref, new_ref, win_ref, ocmp_ref, osel_ref, gate_ref, o_ref, *, n_new):
    n_rows = NSA_HEADS * n_new
    wb = win_ref.shape[1]
    qbd = _stack_nsa(q_ref[...])
    slope = _nsa_slopes(n_rows, n_new)
    t_row = _row_t(n_rows, n_new)
    kv = jnp.concatenate([win_ref[...], new_ref[...]], axis=1).astype(BF16)
    dist = (wb + t_row - _iota((1, wb + PAGE_SIZE), 1)).astype(F32)
    sc = jnp.dot(qbd, kv[0:128], preferred_element_type=F32) - slope * dist
    valid = jnp.logical_and(dist >= 0.0, dist < float(WINDOW))
    p = _masked_softmax_rows(sc, valid)
    o_win = lax.dot_general(p.astype(BF16), kv[128:256], NT, preferred_element_type=F32)
    gates = gate_ref[...]
    for h in range(NSA_HEADS):
        cols = slice(h * LANES, (h + 1) * LANES)
        o_ref[:, cols] = (gates[:, GATE_LANE + h:GATE_LANE + h + 1] * ocmp_ref[:, cols]
                          + gates[:, GATE_LANE + NSA_HEADS + h:GATE_LANE + NSA_HEADS + h + 1] * osel_ref[:, cols]
                          + gates[:, GATE_LANE + 2 * NSA_HEADS + h:GATE_LANE + 2 * NSA_HEADS + h + 1]
                          * o_win[h * n_new:(h + 1) * n_new])


def _win_sample(nsaq, new_t, win_t, o_cmp, o_sel, small, n_new):
    n_seq, _, wb = win_t.shape
    row = lambda b: (b, 0)
    return pl.pallas_call(
        functools.partial(_win_sample_kernel, n_new=n_new),
        grid=(n_seq,),
        in_specs=[pl.BlockSpec((n_new, 1024), row),
                  pl.BlockSpec((None, KV_ROWS_NSA, PAGE_SIZE), lambda b: (b, 0, 0)),
                  pl.BlockSpec((None, KV_ROWS_NSA, wb), lambda b: (b, 0, 0)),
                  pl.BlockSpec((n_new, 1024), row), pl.BlockSpec((n_new, 1024), row),
                  pl.BlockSpec((n_new, LANES), row)],
        out_specs=pl.BlockSpec((n_new, 1024), row),
        out_shape=jax.ShapeDtypeStruct((n_seq * n_new, 1024), F32),
        compiler_params=_cparams(("arbitrary",)), name="win_sample",
    )(nsaq, new_t, win_t, o_cmp, o_sel, small)


def _layer_weights(l, w_in, b_forget, cmp_w, w_br_sb, w_br_nsa, w_br_fox, w_out, tm):
    w = w_in[l]
    scale = HEAD_DIM ** -0.5
    zeros64 = jnp.zeros((D_MODEL, HEAD_DIM), F32)
    nsa_q = w[:, 768:1280] * scale
    pads = []
    for h in range(NSA_HEADS):
        qh = nsa_q[:, h * HEAD_DIM:(h + 1) * HEAD_DIM]
        pads += [qh, zeros64] if h < NSA_GROUP_SIZE else [zeros64, qh]
    wq = jnp.concatenate([w[:, 0:256] * scale] + pads + [w[:, 2072:2328] * scale], axis=1)
    wkv = jnp.concatenate([w[:, 256:768], w[:, 2328:2840], w[:, 1280:1536], w[:, 1536:1792], w[:, 1792:2048]], axis=1)
    wsm = jnp.concatenate([w[:, 2048:2072], w[:, 2840:2844], jnp.zeros((D_MODEL, LANES - 28), F32)], axis=1)
    brow = jnp.zeros((1, LANES), F32).at[0, LOGF_LANE:LOGF_LANE + FOX_HEADS].set(b_forget[l])
    wn = w_br_nsa[l].reshape(NSA_HEADS, HEAD_DIM, D_MODEL)
    zrow = jnp.zeros((HEAD_DIM, D_MODEL), F32)
    rows = []
    for h in range(NSA_HEADS):
        rows += [wn[h], zrow] if h < NSA_GROUP_SIZE else [zrow, wn[h]]
    wkn = jnp.concatenate([w[:, 2328:2584], w[:, 1536:1664], w[:, 1792:1920]], axis=1)
    return dict(
        wq=wq.astype(BF16), wkvt=wkv.T.astype(BF16), wqt=wq[:, 256:1536].T.astype(BF16), wkn=wkn.astype(BF16),
        wcmp=w[:, 1280:1536].astype(BF16), wsm=wsm.astype(BF16),
        brow=brow, cw=jnp.broadcast_to(jnp.tile(cmp_w[l], tm // CMP_BLOCK)[:, None], (tm, KV_ROWS_NSA)),
        wgate=w[:, 2844:5916].astype(BF16), wbr_sb=w_br_sb[l].astype(BF16),
        wbr_nsa=jnp.concatenate(rows, axis=0).astype(BF16), wbr_fox=w_br_fox[l].astype(BF16),
        wout=w_out[l].astype(BF16))


def _cache_t(cache):
    d, n_pool, page = cache.shape[:3]
    return jnp.transpose(cache, (0, 1, 3, 4, 5, 2)).reshape(d, n_pool, -1, page)


def _new_pages(kvt_rows, n_seq, n_new):
    r = kvt_rows.shape[0]
    x = jnp.transpose(kvt_rows.reshape(r, n_seq, n_new), (1, 0, 2))
    return jnp.pad(x, ((0, 0), (0, 0), (0, PAGE_SIZE - n_new)))


def _kv_out(kvt_rows, lead_shape, heads):
    n = kvt_rows.shape[1]
    return jnp.transpose(kvt_rows.reshape(2, heads, HEAD_DIM, n), (3, 0, 1, 2)).reshape(*lead_shape, 2, heads, HEAD_DIM)


def kernel(x_prompt, x_sample, cache_sb_kv, cache_cmp_kv, cache_sel_kv, cache_fox_kv, cache_fox_logf, state_win_kv, page_table, ffn1_pre_g, ffn1_post_g, ffn1_w_gu, ffn1_w_down, mix_pre_g, mix_post_g, w_in, b_forget, cmp_w, w_br_sb, w_br_nsa, w_br_fox, w_out, ffn2_pre_g, ffn2_post_g, ffn2_w_gu, ffn2_w_down):
    depth = w_in.shape[0]
    batch, seq, _ = x_prompt.shape
    n_seq, n_new, _ = x_sample.shape
    past_len = page_table.shape[1] * PAGE_SIZE
    assert batch == 1 and seq % ATT_TILE == 0 and seq >= WINDOW + ATT_TILE
    assert past_len % SEL_BLOCK == 0 and n_new <= CMP_BLOCK and page_table.shape[1] % PAGES_PER_STEP == 0
    m_s = n_seq * n_new
    tm_s = m_s if m_s <= ATT_TILE else ATT_TILE
    assert m_s % tm_s == 0 and tm_s % CMP_BLOCK == 0

    x_p = x_prompt.reshape(seq, D_MODEL)
    x_s = x_sample.reshape(m_s, D_MODEL)
    sb_c, cmp_c, sel_c, fox_c = (_cache_t(c) for c in (cache_sb_kv, cache_cmp_kv, cache_sel_kv, cache_fox_kv))
    lf_c = jnp.transpose(cache_fox_logf, (0, 1, 3, 2))
    wb = state_win_kv.shape[2]
    win_c = jnp.transpose(state_win_kv, (0, 1, 3, 4, 5, 2)).reshape(depth, n_seq, KV_ROWS_NSA, wb)
    row = lambda a, l: a[l][None, :]
    outs = {k: [] for k in ("sb_p", "sb_s", "cmp_p", "cmp_s", "sel_p", "sel_s", "fox_p", "fox_s", "lf_p", "lf_s",
                            "win_p", "win_s")}
    ffn_tm_p = 512 if seq % 512 == 0 else ATT_TILE
    for l in range(depth):
        lw = _layer_weights(l, w_in, b_forget, cmp_w, w_br_sb, w_br_nsa, w_br_fox, w_out, ATT_TILE)
        lw_s = dict(lw, cw=lw["cw"][:tm_s])
        wgu1, wd1 = ffn1_w_gu[l].astype(BF16), ffn1_w_down[l].astype(BF16)
        wgu2, wd2 = ffn2_w_gu[l].astype(BF16), ffn2_w_down[l].astype(BF16)
        x_p = _ffn(x_p, row(ffn1_pre_g, l), row(ffn1_post_g, l), wgu1, wd1, ffn_tm_p)
        x_s = _ffn(x_s, row(ffn1_pre_g, l), row(ffn1_post_g, l), wgu1, wd1, tm_s)

        pp = _proj(x_p, row(mix_pre_g, l), lw, ATT_TILE)
        ck = pp["ck"].reshape(seq // CMP_BLOCK, KV_ROWS_NSA)
        o_sb = _sb_prompt(pp["sbq"], pp["kvt3"], Q_TILE)
        o_fox = _fox_prompt(pp["qt"], pp["kn"], pp["kvt3"], pp["fcum"], pp["kmax"], Q_TILE, FOX_KEYS)
        o_cmp, sel_blocks, flags = _nsa_select(pp["qt"], ck, jnp.transpose(ck), Q_TILE, ATT_TILE)
        o_nsa = _nsa_attend(flags.reshape(-1), pp["qt"], pp["kn"], pp["kvt3"], sel_blocks, o_cmp, pp["gt3"], Q_TILE)
        x_p = _merge(x_p, o_sb, o_nsa, o_fox, row(mix_pre_g, l), row(mix_post_g, l), lw, ATT_TILE)

        ps = _proj(x_s, row(mix_pre_g, l), lw_s, tm_s)
        kvt_s = ps["kvt"]
        new = lambda r0, n: _new_pages(kvt_s[r0:r0 + n], n_seq, n_new)
        lf_new = _new_pages(jnp.transpose(ps["small"][:, LOGF_LANE:LOGF_LANE + FOX_HEADS]), n_seq, n_new)
        o_sb = _sb_sample(ps["sbq"], new(ROW_SB, KV_ROWS_SB), sb_c, page_table, l, n_new)
        o_fox = _fox_sample(ps["foxq"], new(ROW_FOX, KV_ROWS_FOX), lf_new, fox_c, lf_c, page_table, l, n_new)
        step_keys = PAGES_PER_STEP * PAGE_SIZE
        cw_keys = jnp.broadcast_to(jnp.tile(cmp_w[l], step_keys // CMP_BLOCK)[:, None],
                                   (step_keys, min(LANES, past_len // CMP_BLOCK)))
        o_cmp, sel = _cmp_sample(ps["nsaq"], cw_keys, cmp_c, page_table, l, n_new, past_len)
        o_sel = _sel_sample(ps["nsaq"], new(ROW_SEL, KV_ROWS_NSA), sel, sel_c, page_table, l, n_new, past_len)
        new_win = new(ROW_WIN, KV_ROWS_NSA)
        o_nsa = _win_sample(ps["nsaq"], new_win, win_c[l], o_cmp, o_sel, ps["small"], n_new)
        x_s = _merge(x_s, o_sb, o_nsa, o_fox, row(mix_pre_g, l), row(mix_post_g, l), lw, tm_s)

        x_p = _ffn(x_p, row(ffn2_pre_g, l), row(ffn2_post_g, l), wgu2, wd2, ffn_tm_p)
        x_s = _ffn(x_s, row(ffn2_pre_g, l), row(ffn2_post_g, l), wgu2, wd2, tm_s)

        kvt_p = pp["kvt"]
        outs["sb_p"].append(_kv_out(kvt_p[ROW_SB:ROW_SB + 512], (batch, seq), SB_HEADS))
        outs["sb_s"].append(_kv_out(kvt_s[ROW_SB:ROW_SB + 512], (n_seq, n_new), SB_HEADS))
        outs["cmp_p"].append(_kv_out(kvt_p[ROW_CMP:ROW_CMP + 256], (batch, seq), NSA_KV_GROUPS))
        outs["cmp_s"].append(_kv_out(kvt_s[ROW_CMP:ROW_CMP + 256], (n_seq, n_new), NSA_KV_GROUPS))
        outs["sel_p"].append(_kv_out(kvt_p[ROW_SEL:ROW_SEL + 256], (batch, seq), NSA_KV_GROUPS))
        outs["sel_s"].append(_kv_out(kvt_s[ROW_SEL:ROW_SEL + 256], (n_seq, n_new), NSA_KV_GROUPS))
        outs["fox_p"].append(_kv_out(kvt_p[ROW_FOX:ROW_FOX + 512], (batch, seq), FOX_HEADS))
        outs["fox_s"].append(_kv_out(kvt_s[ROW_FOX:ROW_FOX + 512], (n_seq, n_new), FOX_HEADS))
        outs["lf_p"].append(pp["small"][:, LOGF_LANE:LOGF_LANE + FOX_HEADS].reshape(batch, seq, FOX_HEADS))
        outs["lf_s"].append(ps["small"][:, LOGF_LANE:LOGF_LANE + FOX_HEADS].reshape(n_seq, n_new, FOX_HEADS))
        w_keep = min(WINDOW, seq)
        outs["win_p"].append(_kv_out(kvt_p[ROW_WIN:ROW_WIN + 256, seq - w_keep:], (batch, w_keep), NSA_KV_GROUPS))
        win_all = jnp.concatenate([win_c[l], new_win[:, :, :n_new]], axis=2)[:, :, n_new:]
        outs["win_s"].append(jnp.transpose(win_all.reshape(n_seq, 2, NSA_KV_GROUPS, HEAD_DIM, wb), (0, 4, 1, 2, 3)))

    st = lambda k: jnp.stack(outs[k])
    return (x_p.reshape(batch, seq, D_MODEL), x_s.reshape(n_seq, n_new, D_MODEL),
            st("sb_p"), st("sb_s"), st("cmp_p"), st("cmp_s"), st("sel_p"), st("sel_s"),
            st("fox_p"), st("fox_s"), st("lf_p"), st("lf_s"), st("win_p"), st("win_s"))
```

```python
import functools

import jax
import jax.numpy as jnp
from jax import lax
from jax.experimental import pallas as pl
from jax.experimental.pallas import tpu as pltpu

F32 = jnp.float32
BF16 = jnp.bfloat16

D_MODEL = 1024
HEAD_DIM = 64
SB_HEADS = 4
NSA_HEADS = 8
NSA_KV_GROUPS = 2
NSA_GROUP_SIZE = NSA_HEADS // NSA_KV_GROUPS
FOX_HEADS = 4
CMP_BLOCK = 64
SEL_BLOCK = 64
SEL_TOPK = 16
WINDOW = 512
D_FF = 2816
PAGE_SIZE = 128
RMS_EPS = 1e-6
NEG_INF = -1e30
FORCE_SCORE = 1e4

LANES = 128
V7X_VMEM_LIMIT_BYTES = 56 * 1024 * 1024
ATT_TILE = 256
Q_TILE = 256
FOX_KEYS = 512
FFN_CHUNK = 1408
PAGES_PER_STEP = 16
SB_EXIT = 110.0

KV_ROWS_SB = 2 * SB_HEADS * HEAD_DIM
KV_ROWS_FOX = 2 * FOX_HEADS * HEAD_DIM
KV_ROWS_NSA = 2 * NSA_KV_GROUPS * HEAD_DIM
ROW_SB, ROW_FOX, ROW_CMP, ROW_SEL, ROW_WIN = 0, 512, 1024, 1280, 1536
KV_ROWS_ALL = 1792
QT_ROWS = 1280
KN_COLS = 512
EXP_ZERO = 110.0
GATE_LANE = 0
LOGF_LANE = 24

NT = (((1,), (1,)), ((), ()))


def _cparams(sem):
    return pltpu.CompilerParams(dimension_semantics=sem, vmem_limit_bytes=V7X_VMEM_LIMIT_BYTES)


def _rms(x, g):
    return x * lax.rsqrt(jnp.mean(x * x, axis=-1, keepdims=True) + RMS_EPS) * g


def _softplus(z):
    return jnp.maximum(z, 0.0) + jnp.log(1.0 + jnp.exp(-jnp.abs(z)))


def _sigmoid(z):
    return 1.0 / (1.0 + jnp.exp(-z))


def _split_bf16(x, n):
    parts = []
    r = x
    for _ in range(n):
        p = r.astype(BF16)
        parts.append(p)
        r = r - p.astype(F32)
    return parts


def _dot_split_lhs(x, u, n):
    return sum(jnp.dot(p, u, preferred_element_type=F32) for p in _split_bf16(x, n))


def _dot_split_rhs(u, x, n):
    return sum(jnp.dot(u, p, preferred_element_type=F32) for p in _split_bf16(x, n))


def _iota(shape, dim):
    return lax.broadcasted_iota(jnp.int32, shape, dim)


def _stack_heads(q, n_heads):
    lane_head = _iota(q.shape, 1) // HEAD_DIM
    return jnp.concatenate([jnp.where(lane_head == h, q, 0.0) for h in range(n_heads)], axis=0).astype(BF16)


def _stack_nsa(qpad):
    return jnp.concatenate([qpad[:, h * LANES:(h + 1) * LANES] for h in range(NSA_HEADS)], axis=0).astype(BF16)


def _diag_select(o_all, n_heads, tq):
    lane_head = _iota((tq, o_all.shape[1]), 1) // HEAD_DIM
    out = o_all[0:tq]
    for h in range(1, n_heads):
        out = jnp.where(lane_head == h, o_all[h * tq:(h + 1) * tq], out)
    return out


def _row_t(n_rows, tq):
    return _iota((n_rows, 1), 0) % tq


def _nsa_slopes(n_rows, tq):
    head = _iota((n_rows, 1), 0) // tq
    slope = jnp.full((n_rows, 1), 0.5, F32)
    for h in range(1, NSA_HEADS):
        slope = jnp.where(head == h, 2.0 ** -(h + 1), slope)
    return slope


def _flash_update(s, v_t, m_ref, l_ref, acc_ref):
    m_old = m_ref[...]
    m_new = jnp.maximum(m_old, jnp.max(s, axis=1, keepdims=True))
    alpha = jnp.exp(m_old - m_new)
    p = jnp.exp(s - m_new)
    l_ref[...] = alpha * l_ref[...] + jnp.sum(p, axis=1, keepdims=True)
    acc_ref[...] = alpha * acc_ref[...] + lax.dot_general(p.astype(BF16), v_t, NT, preferred_element_type=F32)
    m_ref[...] = m_new


def _flash_init(m_ref, l_ref, acc_ref):
    m_ref[...] = jnp.full(m_ref.shape, NEG_INF, F32)
    l_ref[...] = jnp.zeros(l_ref.shape, F32)
    acc_ref[...] = jnp.zeros(acc_ref.shape, F32)


def _flash_out(l_ref, acc_ref):
    return acc_ref[...] / jnp.maximum(l_ref[...], 1e-30)


def _flash_update_t(s, v_t, m_ref, l_ref, acc_ref):
    m_old = m_ref[...]
    m_new = jnp.maximum(m_old, jnp.max(s, axis=0, keepdims=True))
    alpha = jnp.exp(m_old - m_new)
    p = jnp.exp(s - m_new)
    l_ref[...] = alpha * l_ref[...] + jnp.sum(p, axis=0, keepdims=True)
    acc_ref[...] = alpha * acc_ref[...] + jnp.dot(v_t, p.astype(BF16), preferred_element_type=F32)
    m_ref[...] = m_new


def _top_blocks_t(score, n_pick):
    nb = score.shape[0]
    blk = _iota(score.shape, 0).astype(F32)
    sel = jnp.zeros(score.shape, F32)
    for _ in range(n_pick):
        m = jnp.max(score, axis=0, keepdims=True)
        first = jnp.min(jnp.where(score == m, blk, float(nb)), axis=0, keepdims=True)
        hit = blk == first
        sel = jnp.where(hit, 1.0, sel)
        score = jnp.where(hit, -jnp.inf, score)
    return sel


def _masked_softmax_rows(s, valid):
    s = jnp.where(valid, s, NEG_INF)
    m = jnp.max(s, axis=1, keepdims=True)
    p = jnp.where(valid, jnp.exp(s - m), 0.0)
    return p / jnp.maximum(jnp.sum(p, axis=1, keepdims=True), 1e-30)


def _top_blocks(score, n_pick):
    nb = score.shape[1]
    blk = _iota(score.shape, 1).astype(F32)
    sel = jnp.zeros(score.shape, F32)
    for _ in range(n_pick):
        m = jnp.max(score, axis=1, keepdims=True)
        first = jnp.min(jnp.where(score == m, blk, float(nb)), axis=1, keepdims=True)
        hit = blk == first
        sel = jnp.where(hit, 1.0, sel)
        score = jnp.where(hit, -jnp.inf, score)
    return sel


def _sb_step(qbd, k_t, v_t, u, valid, carry_ref, acc_ref):
    z = jnp.dot(qbd, k_t, preferred_element_type=F32)
    log_keep = -_softplus(z)
    if valid is not None:
        log_keep = jnp.where(valid, log_keep, 0.0)
    after = _dot_split_lhs(log_keep, u, 2)
    carry = carry_ref[...]
    w = jnp.exp(z + log_keep + after + carry)
    if valid is not None:
        w = jnp.where(valid, w, 0.0)
    acc_ref[...] += lax.dot_general(w.astype(BF16), v_t, NT, preferred_element_type=F32)
    carry_ref[...] = carry + after[:, 0:1] + log_keep[:, 0:1]


def _strict_upper(n):
    return jnp.where(_iota((n, n), 0) > _iota((n, n), 1), 1.0, 0.0).astype(BF16)


def _ffn_kernel(x_ref, pre_ref, post_ref, wgu_ref, wd_ref, o_ref):
    x = x_ref[...]
    h = _rms(x, pre_ref[...]).astype(BF16)
    acc = jnp.zeros(x.shape, F32)
    for c in range(D_FF // FFN_CHUNK):
        lo = c * FFN_CHUNK
        gate = jnp.dot(h, wgu_ref[:, lo:lo + FFN_CHUNK], preferred_element_type=F32)
        up = jnp.dot(h, wgu_ref[:, D_FF + lo:D_FF + lo + FFN_CHUNK], preferred_element_type=F32)
        a = (gate * _sigmoid(gate) * up).astype(BF16)
        acc = acc + jnp.dot(a, wd_ref[lo:lo + FFN_CHUNK, :], preferred_element_type=F32)
    o_ref[...] = x + 0.5 * _rms(acc, post_ref[...])


def _ffn(x, pre_g, post_g, w_gu, w_down, tm):
    m = x.shape[0]
    const = lambda i: (0, 0)
    return pl.pallas_call(
        _ffn_kernel,
        grid=(m // tm,),
        in_specs=[pl.BlockSpec((tm, D_MODEL), lambda i: (i, 0)),
                  pl.BlockSpec((1, D_MODEL), const), pl.BlockSpec((1, D_MODEL), const),
                  pl.BlockSpec((D_MODEL, 2 * D_FF), const, pipeline_mode=pl.Buffered(1)),
                  pl.BlockSpec((D_FF, D_MODEL), const, pipeline_mode=pl.Buffered(1))],
        out_specs=pl.BlockSpec((tm, D_MODEL), lambda i: (i, 0)),
        out_shape=jax.ShapeDtypeStruct((m, D_MODEL), F32),
        compiler_params=_cparams(("arbitrary",)),
        name="half_ffn",
    )(x, pre_g, post_g, w_gu, w_down)


def _proj_kernel(x_ref, g_ref, wq_ref, wkvt_ref, wqt_ref, wkn_ref, wcmp_ref, wsm_ref, brow_ref, cw_ref,
                 sbq_ref, nsaq_ref, foxq_ref, kvt_ref, kvt3_ref, qt_ref, kn_ref, ck_ref, small_ref, fcum_ref, gt3_ref,
                 kmax_ref, carry_ref, kmx_ref, *, tm):
    @pl.when(pl.program_id(0) == 0)
    def _():
        carry_ref[...] = jnp.zeros(carry_ref.shape, F32)
        kmx_ref[...] = jnp.zeros(kmx_ref.shape, F32)

    h = _rms(x_ref[...], g_ref[...]).astype(BF16)
    q = jnp.dot(h, wq_ref[...], preferred_element_type=F32)
    sbq_ref[...] = q[:, 0:256]
    nsaq_ref[...] = q[:, 256:1280]
    foxq_ref[...] = q[:, 1280:1536]
    kvt = lax.dot_general(wkvt_ref[...], h, NT, preferred_element_type=F32)
    kvt_ref[...] = kvt
    kvt3_ref[...] = kvt.astype(BF16)
    qt_ref[...] = lax.dot_general(wqt_ref[...], h, NT, preferred_element_type=F32).astype(BF16)
    kn = jnp.dot(h, wkn_ref[...], preferred_element_type=F32)
    kn_ref[...] = kn.astype(BF16)
    kf = kn[:, 0:256]
    head_of = jnp.where(_iota((256, LANES), 0) // HEAD_DIM == _iota((256, LANES), 1), 1.0, 0.0).astype(BF16)
    norm2 = jnp.max(jnp.dot((kf * kf).astype(BF16), head_of, preferred_element_type=F32), axis=0, keepdims=True)
    kmx_ref[...] = jnp.maximum(kmx_ref[...], norm2)
    kmax_ref[...] = kmx_ref[...]
    ycmp = jnp.dot(h, wcmp_ref[...], preferred_element_type=F32)
    ck_ref[...] = (ycmp * cw_ref[...]).reshape(tm // CMP_BLOCK, CMP_BLOCK, KV_ROWS_NSA).sum(axis=1)
    ys = jnp.dot(h, wsm_ref[...], preferred_element_type=F32)
    lane = _iota((1, LANES), 1)
    log_f = -_softplus(-(ys + brow_ref[...]))
    is_f = jnp.logical_and(lane >= LOGF_LANE, lane < LOGF_LANE + FOX_HEADS)
    small = jnp.where(lane < LOGF_LANE, _sigmoid(ys), jnp.where(is_f, log_f, 0.0))
    small_ref[...] = small
    gq = gt3_ref.shape[2]
    for j in range(tm // gq):
        gt3_ref[j] = small[j * gq:(j + 1) * gq].T
    tri = jnp.where(_iota((tm, tm), 0) >= _iota((tm, tm), 1), 1.0, 0.0).astype(BF16)
    cum = _dot_split_rhs(tri, jnp.where(is_f, log_f, 0.0), 3) + carry_ref[...]
    carry_ref[...] = cum[tm - 1:tm, :]
    fcum_ref[...] = cum


def _proj(x, g, lw, tm):
    m = x.shape[0]
    n = m // tm
    const = lambda i: (0, 0)
    row = lambda i: (i, 0)
    gq = min(Q_TILE, tm)
    outs = pl.pallas_call(
        functools.partial(_proj_kernel, tm=tm),
        grid=(n,),
        in_specs=[pl.BlockSpec((tm, D_MODEL), row), pl.BlockSpec((1, D_MODEL), const),
                  pl.BlockSpec((D_MODEL, 1536), const), pl.BlockSpec((KV_ROWS_ALL, D_MODEL), const),
                  pl.BlockSpec((QT_ROWS, D_MODEL), const), pl.BlockSpec((D_MODEL, KN_COLS), const),
                  pl.BlockSpec((D_MODEL, KV_ROWS_NSA), const), pl.BlockSpec((D_MODEL, LANES), const),
                  pl.BlockSpec((1, LANES), const), pl.BlockSpec((tm, KV_ROWS_NSA), const)],
        out_specs=[pl.BlockSpec((tm, 256), row), pl.BlockSpec((tm, 1024), row), pl.BlockSpec((tm, 256), row),
                   pl.BlockSpec((KV_ROWS_ALL, tm), lambda i: (0, i)),
                   pl.BlockSpec((None, KV_ROWS_ALL, tm), lambda i: (i, 0, 0)),
                   pl.BlockSpec((QT_ROWS, tm), lambda i: (0, i)),
                   pl.BlockSpec((tm, KN_COLS), row),
                   pl.BlockSpec((None, tm // CMP_BLOCK, KV_ROWS_NSA), lambda i: (i, 0, 0)),
                   pl.BlockSpec((tm, LANES), row), pl.BlockSpec((tm, LANES), row),
                   pl.BlockSpec((tm // gq, LANES, gq), lambda i: (i, 0, 0)),
                   pl.BlockSpec((1, LANES), const)],
        out_shape=[jax.ShapeDtypeStruct((m, 256), F32), jax.ShapeDtypeStruct((m, 1024), F32),
                   jax.ShapeDtypeStruct((m, 256), F32),
                   jax.ShapeDtypeStruct((KV_ROWS_ALL, m), F32),
                   jax.ShapeDtypeStruct((n, KV_ROWS_ALL, tm), BF16),
                   jax.ShapeDtypeStruct((QT_ROWS, m), BF16),
                   jax.ShapeDtypeStruct((m, KN_COLS), BF16),
                   jax.ShapeDtypeStruct((n, tm // CMP_BLOCK, KV_ROWS_NSA), F32),
                   jax.ShapeDtypeStruct((m, LANES), F32), jax.ShapeDtypeStruct((m, LANES), F32),
                   jax.ShapeDtypeStruct((m // gq, LANES, gq), F32),
                   jax.ShapeDtypeStruct((1, LANES), F32)],
        scratch_shapes=[pltpu.VMEM((1, LANES), F32), pltpu.VMEM((1, LANES), F32)],
        compiler_params=_cparams(("arbitrary",)),
        name="mixer_proj",
    )(x, g, lw["wq"], lw["wkvt"], lw["wqt"], lw["wkn"], lw["wcmp"], lw["wsm"], lw["brow"], lw["cw"])
    keys = ("sbq", "nsaq", "foxq", "kvt", "kvt3", "qt", "kn", "ck", "small", "fcum", "gt3", "kmax")
    return dict(zip(keys, outs))


def _merge_kernel(x_ref, osb_ref, onsa_ref, ofox_ref, pre_ref, post_ref, wg_ref, wsb_ref, wnsa_ref, wfox_ref,
                  wout_ref, o_ref):
    x = x_ref[...]
    h = _rms(x, pre_ref[...]).astype(BF16)
    mixed = jnp.zeros(x.shape, F32)
    for b, (o_b, w_b) in enumerate(((osb_ref, wsb_ref), (onsa_ref, wnsa_ref), (ofox_ref, wfox_ref))):
        gate = _sigmoid(jnp.dot(h, wg_ref[:, b * D_MODEL:(b + 1) * D_MODEL], preferred_element_type=F32))
        mixed = mixed + gate * jnp.dot(o_b[...].astype(BF16), w_b[...], preferred_element_type=F32)
    y = jnp.dot(mixed.astype(BF16), wout_ref[...], preferred_element_type=F32)
    o_ref[...] = x + _rms(y, post_ref[...])


def _merge(x, o_sb, o_nsa, o_fox, pre_g, post_g, lw, tm):
    m = x.shape[0]
    const = lambda i: (0, 0)
    row = lambda i: (i, 0)
    return pl.pallas_call(
        _merge_kernel,
        grid=(m // tm,),
        in_specs=[pl.BlockSpec((tm, D_MODEL), row), pl.BlockSpec((tm, 256), row), pl.BlockSpec((tm, 1024), row),
                  pl.BlockSpec((tm, 256), row), pl.BlockSpec((1, D_MODEL), const), pl.BlockSpec((1, D_MODEL), const),
                  pl.BlockSpec((D_MODEL, 3 * D_MODEL), const), pl.BlockSpec((256, D_MODEL), const),
                  pl.BlockSpec((1024, D_MODEL), const), pl.BlockSpec((256, D_MODEL), const),
                  pl.BlockSpec((D_MODEL, D_MODEL), const)],
        out_specs=pl.BlockSpec((tm, D_MODEL), row),
        out_shape=jax.ShapeDtypeStruct((m, D_MODEL), F32),
        compiler_params=_cparams(("arbitrary",)),
        name="mixer_merge",
    )(x, o_sb, o_nsa, o_fox, pre_g, post_g, lw["wgate"], lw["wbr_sb"], lw["wbr_nsa"], lw["wbr_fox"], lw["wout"])


def _sb_prompt_kernel(q_ref, kv_ref, o_ref, carry_ref, acc_ref, *, tq, tk):
    i = pl.program_id(0)
    n_rows = SB_HEADS * tq
    qbd = _stack_heads(q_ref[...], SB_HEADS)
    u = _strict_upper(tk)
    kb0 = (i * tq) // tk
    t_pos = i * tq + _row_t(n_rows, tq)
    carry_ref[...] = jnp.zeros(carry_ref.shape, F32)
    acc_ref[...] = jnp.zeros(acc_ref.shape, F32)

    def tile(kb, diagonal):
        k_t = kv_ref[kb, 0:256, :]
        v_t = kv_ref[kb, 256:512, :]
        valid = (kb * tk + _iota((1, tk), 1) < t_pos) if diagonal else None
        _sb_step(qbd, k_t, v_t, u, valid, carry_ref, acc_ref)

    tile(kb0, True)

    def cond(c):
        return jnp.logical_and(c[0] >= 0, c[1] > -SB_EXIT)

    def body(c):
        tile(c[0], False)
        return c[0] - 1, jnp.max(carry_ref[...])

    lax.while_loop(cond, body, (kb0 - 1, jnp.max(carry_ref[...])))
    o_ref[...] = _diag_select(acc_ref[...], SB_HEADS, tq).astype(o_ref.dtype)


def _sb_prompt(sbq, kvt3, tq):
    t = sbq.shape[0]
    n_kt, _, tk = kvt3.shape
    return pl.pallas_call(
        functools.partial(_sb_prompt_kernel, tq=tq, tk=tk),
        grid=(t // tq,),
        in_specs=[pl.BlockSpec((tq, 256), lambda i: (i, 0)),
                  pl.BlockSpec((n_kt, KV_ROWS_SB, tk), lambda i: (0, ROW_SB // KV_ROWS_SB, 0))],
        out_specs=pl.BlockSpec((tq, 256), lambda i: (i, 0)),
        out_shape=jax.ShapeDtypeStruct((t, 256), BF16),
        scratch_shapes=[pltpu.VMEM((SB_HEADS * tq, 1), F32), pltpu.VMEM((SB_HEADS * tq, 256), F32)],
        compiler_params=_cparams(("arbitrary",)),
        name="sb_prompt",
    )(sbq, kvt3)


def _lanes_per_head(row, n_heads, tq, lane0):
    return jnp.concatenate([jnp.broadcast_to(row[:, lane0 + h:lane0 + h + 1], (1, tq)) for h in range(n_heads)], axis=1)


def _fox_prompt_kernel(qt_ref, k_ref, vt_ref, fcum_ref, kmax_ref, o_ref, m_ref, l_ref, acc_ref, *, tq, tk):
    i = pl.program_id(0)
    n_lanes = FOX_HEADS * tq
    qt = qt_ref[...]
    row_head = _iota(qt.shape, 0) // HEAD_DIM
    qbd_t = jnp.concatenate([jnp.where(row_head == h, qt, jnp.zeros_like(qt)) for h in range(FOX_HEADS)], axis=1)
    kb0 = (i * tq) // tk
    t_pos = i * tq + _iota((1, n_lanes), 1) % tq
    q32 = qbd_t.astype(F32)
    q_norm = jnp.sqrt(jnp.sum(q32 * q32, axis=0, keepdims=True))
    z_bound = 1.02 * q_norm * jnp.sqrt(_lanes_per_head(kmax_ref[...], FOX_HEADS, tq, 0))
    _flash_init(m_ref, l_ref, acc_ref)

    v_tiles = tk // vt_ref.shape[2]

    def tile(kb, diagonal):
        start = pl.multiple_of(kb * tk, tk)
        f = fcum_ref[pl.ds(start, tk), :]
        fk = jnp.concatenate([jnp.broadcast_to(f[:, LOGF_LANE + h:LOGF_LANE + h + 1], (tk, tq))
                              for h in range(FOX_HEADS)], axis=1)
        s = jnp.dot(k_ref[pl.ds(start, tk), :], qbd_t, preferred_element_type=F32) - fk
        if diagonal:
            s = jnp.where(kb * tk + _iota((tk, n_lanes), 0) <= t_pos, s, NEG_INF)
        v_t = jnp.concatenate([vt_ref[kb * v_tiles + j] for j in range(v_tiles)], axis=1)
        _flash_update_t(s, v_t, m_ref, l_ref, acc_ref)

    def live(kb):
        f_end = fcum_ref[pl.ds(kb * tk + tk - 1, 1), :]
        bound = z_bound - _lanes_per_head(f_end, FOX_HEADS, tq, LOGF_LANE)
        return jnp.max(bound - m_ref[...]) > -EXP_ZERO

    tile(kb0, True)

    def cond(c):
        return c[1]

    def body(c):
        tile(c[0], False)
        nxt = c[0] - 1
        return nxt, jnp.logical_and(nxt >= 0, live(jnp.maximum(nxt, 0)))

    first = kb0 - 1
    lax.while_loop(cond, body, (first, jnp.logical_and(first >= 0, live(jnp.maximum(first, 0)))))
    o_t = _flash_out(l_ref, acc_ref)
    o_t = jnp.concatenate([o_t[h * HEAD_DIM:(h + 1) * HEAD_DIM, h * tq:(h + 1) * tq] for h in range(FOX_HEADS)], axis=0)
    o_ref[...] = o_t.T.astype(o_ref.dtype)


def _resident(shape, index):
    return pl.BlockSpec(shape, index, pipeline_mode=pl.Buffered(1))


def _fox_prompt(qt, kn, kvt3, fcum, kmax, tq, tk):
    t = kn.shape[0]
    n_vt, _, tv = kvt3.shape
    n_lanes = FOX_HEADS * tq
    assert tk % tv == 0 and tk % tq == 0 and t % tk == 0
    return pl.pallas_call(
        functools.partial(_fox_prompt_kernel, tq=tq, tk=tk),
        grid=(t // tq,),
        in_specs=[pl.BlockSpec((256, tq), lambda i: (4, i)),
                  _resident((t, 256), lambda i: (0, 0)),
                  _resident((n_vt, 256, tv), lambda i: (0, (ROW_FOX + 256) // 256, 0)),
                  _resident((t, LANES), lambda i: (0, 0)),
                  pl.BlockSpec((1, LANES), lambda i: (0, 0))],
        out_specs=pl.BlockSpec((tq, 256), lambda i: (i, 0)),
        out_shape=jax.ShapeDtypeStruct((t, 256), BF16),
        scratch_shapes=[pltpu.VMEM((1, n_lanes), F32), pltpu.VMEM((1, n_lanes), F32), pltpu.VMEM((256, n_lanes), F32)],
        compiler_params=_cparams(("arbitrary",)),
        name="fox_prompt",
    )(qt, kn, kvt3, fcum, kmax)


def _group_rows(x_g0, x_g1):
    return jnp.concatenate([x_g0] * NSA_GROUP_SIZE + [x_g1] * NSA_GROUP_SIZE, axis=0)


def _slope_lanes(heads, tq):
    return jnp.concatenate([jnp.full((1, tq), 2.0 ** -(h + 1), F32) for h in heads], axis=1)


def _stack_nsa_t(qt_ref, heads):
    return jnp.concatenate([qt_ref[h * LANES:(h + 1) * LANES, :] for h in heads], axis=1)


def _nsa_select_kernel(qt_ref, ck_ref, ckt_ref, ocmp_ref, sel_ref, flag_ref, *, tq, tk, n_blocks):
    i = pl.program_id(0)
    n_lanes = NSA_HEADS * tq
    qbd_t = _stack_nsa_t(qt_ref, range(NSA_HEADS))
    slope = _slope_lanes(range(NSA_HEADS), tq)
    t_pos = i * tq + _iota((1, n_lanes), 1) % tq
    cmp_end = _iota((n_blocks, n_lanes), 0) * CMP_BLOCK + (CMP_BLOCK - 1)
    s_c = jnp.dot(ck_ref[:, 0:128].astype(BF16), qbd_t, preferred_element_type=F32)
    s_c = s_c - slope * (t_pos - cmp_end).astype(F32)
    valid = cmp_end <= t_pos
    s_c = jnp.where(valid, s_c, NEG_INF)
    p = jnp.where(valid, jnp.exp(s_c - jnp.max(s_c, axis=0, keepdims=True)), 0.0)
    p_c = p / jnp.maximum(jnp.sum(p, axis=0, keepdims=True), 1e-30)
    ocmp_ref[...] = jnp.dot(ckt_ref[128:256, :].astype(BF16), p_c.astype(BF16), preferred_element_type=F32)

    cur = (i * tq + _iota((1, tq), 1)) // SEL_BLOCK
    blk = _iota((n_blocks, tq), 0)
    forced = jnp.logical_or(jnp.logical_or(blk == cur, blk == cur - 1), blk == 0)
    n_kt = flag_ref.shape[1]
    tile_of = jnp.where(_iota((n_kt, n_blocks), 1) // (tk // SEL_BLOCK) == _iota((n_kt, n_blocks), 0), 1.0, 0.0).astype(BF16)
    for g in range(NSA_KV_GROUPS):
        score = sum(p_c[:, (g * NSA_GROUP_SIZE + r) * tq:(g * NSA_GROUP_SIZE + r + 1) * tq] for r in range(NSA_GROUP_SIZE))
        score = jnp.where(forced, FORCE_SCORE, jnp.where(blk > cur, -FORCE_SCORE, score))
        sel = _top_blocks_t(score, min(SEL_TOPK, n_blocks))
        sel_ref[g * n_blocks:(g + 1) * n_blocks, :] = jnp.where(sel > 0.5, 0.0, NEG_INF)
        hits = jnp.dot(tile_of, sel.astype(BF16), preferred_element_type=F32)
        flag_ref[g] = (jnp.max(hits, axis=1, keepdims=True) > 0.5).astype(jnp.int32)


def _nsa_select(qt, ck, ckt, tq, tk):
    t = qt.shape[1]
    n_blocks = ck.shape[0]
    n_qt, n_kt = t // tq, t // tk
    return pl.pallas_call(
        functools.partial(_nsa_select_kernel, tq=tq, tk=tk, n_blocks=n_blocks),
        grid=(n_qt,),
        in_specs=[pl.BlockSpec((NSA_HEADS * LANES, tq), lambda i: (0, i)),
                  pl.BlockSpec((n_blocks, KV_ROWS_NSA), lambda i: (0, 0)),
                  pl.BlockSpec((KV_ROWS_NSA, n_blocks), lambda i: (0, 0))],
        out_specs=[pl.BlockSpec((None, LANES, NSA_HEADS * tq), lambda i: (i, 0, 0)),
                   pl.BlockSpec((None, NSA_KV_GROUPS * n_blocks, tq), lambda i: (i, 0, 0)),
                   pl.BlockSpec((None, NSA_KV_GROUPS, n_kt, 1), lambda i: (i, 0, 0, 0))],
        out_shape=[jax.ShapeDtypeStruct((n_qt, LANES, NSA_HEADS * tq), F32),
                   jax.ShapeDtypeStruct((n_qt, NSA_KV_GROUPS * n_blocks, tq), F32),
                   jax.ShapeDtypeStruct((n_qt, NSA_KV_GROUPS, n_kt, 1), jnp.int32)],
        compiler_params=_cparams(("arbitrary",)),
        name="nsa_select",
    )(qt, ck, ckt)


def _nsa_attend_kernel(flag_ref, qt_ref, ksel_ref, vsel_ref, kwin_ref, vwin_ref, sel_ref, ocmp_ref, gate_ref,
                       o_ref, m_ref, l_ref, acc_ref, *, tq, tk, n_blocks):
    i = pl.program_id(0)
    n_kt = vsel_ref.shape[0]
    n_lanes = NSA_GROUP_SIZE * tq
    kb0 = (i * tq) // tk
    blocks_per_tile = tk // SEL_BLOCK
    tiles_per_chunk = 8 // blocks_per_tile
    assert blocks_per_tile * tiles_per_chunk == 8 and n_blocks % 8 == 0
    rel = (_iota((1, n_lanes), 1) % tq - _iota((tk, n_lanes), 0)).astype(F32)
    gates = gate_ref[...]
    for g in range(NSA_KV_GROUPS):
        heads = range(g * NSA_GROUP_SIZE, (g + 1) * NSA_GROUP_SIZE)
        qbd_t = _stack_nsa_t(qt_ref, heads)
        slope = _slope_lanes(heads, tq)
        bias_rel = slope * rel

        def tile(kb, k_ref, vt_ref, selected, diagonal):
            start = pl.multiple_of(kb * tk, tk)
            off = (i * tq - kb * tk).astype(F32)
            s = jnp.dot(k_ref[pl.ds(start, tk), :], qbd_t, preferred_element_type=F32) - bias_rel - slope * off
            if selected:
                chunk = sel_ref[pl.ds(pl.multiple_of(g * n_blocks + (kb // tiles_per_chunk) * 8, 8), 8), :]
                sub = kb % tiles_per_chunk
                rows = chunk[0:blocks_per_tile]
                for c in range(1, tiles_per_chunk):
                    rows = jnp.where(sub == c, chunk[c * blocks_per_tile:(c + 1) * blocks_per_tile], rows)
                mask = jnp.concatenate([jnp.broadcast_to(rows[b:b + 1, :], (SEL_BLOCK, tq))
                                        for b in range(blocks_per_tile)], axis=0)
                s = s + jnp.concatenate([mask] * NSA_GROUP_SIZE, axis=1)
                if diagonal:
                    s = jnp.where(rel >= -off, s, NEG_INF)
            else:
                dist = rel + off
                s = jnp.where(dist >= 0.0, jnp.where(dist < float(WINDOW), s, NEG_INF), NEG_INF)
            _flash_update_t(s, vt_ref[kb], m_ref, l_ref, acc_ref)

        _flash_init(m_ref, l_ref, acc_ref)

        def sel_body(kb, c):
            @pl.when(flag_ref[(i * NSA_KV_GROUPS + g) * n_kt + kb] > 0)
            def _():
                tile(kb, ksel_ref, vsel_ref, True, False)
            return c

        lax.fori_loop(0, kb0, sel_body, 0)
        tile(kb0, ksel_ref, vsel_ref, True, True)
        o_sel = _flash_out(l_ref, acc_ref)

        _flash_init(m_ref, l_ref, acc_ref)

        def win_body(kb, c):
            tile(kb, kwin_ref, vwin_ref, False, False)
            return c

        lax.fori_loop(jnp.maximum(i * tq - (WINDOW - 1), 0) // tk, kb0 + 1, win_body, 0)
        o_win = _flash_out(l_ref, acc_ref)

        for r, h in enumerate(heads):
            lanes = slice(r * tq, (r + 1) * tq)
            o_h = (gates[GATE_LANE + h:GATE_LANE + h + 1, :] * ocmp_ref[:, h * tq:(h + 1) * tq]
                   + gates[GATE_LANE + NSA_HEADS + h:GATE_LANE + NSA_HEADS + h + 1, :] * o_sel[:, lanes]
                   + gates[GATE_LANE + 2 * NSA_HEADS + h:GATE_LANE + 2 * NSA_HEADS + h + 1, :] * o_win[:, lanes])
            o_ref[:, h * LANES:(h + 1) * LANES] = o_h.T.astype(o_ref.dtype)


def _nsa_attend(flags, qt, kn, kvt3, sel, ocmp, gt3, tq):
    t = kn.shape[0]
    n_kt, _, tk = kvt3.shape
    n_blocks = sel.shape[1] // NSA_KV_GROUPS
    n_lanes = NSA_GROUP_SIZE * tq
    grid_spec = pltpu.PrefetchScalarGridSpec(
        num_scalar_prefetch=1, grid=(t // tq,),
        in_specs=[pl.BlockSpec((NSA_HEADS * LANES, tq), lambda i, f: (0, i)),
                  _resident((t, LANES), lambda i, f: (0, 2)),
                  _resident((n_kt, LANES, tk), lambda i, f: (0, (ROW_SEL + LANES) // LANES, 0)),
                  _resident((t, LANES), lambda i, f: (0, 3)),
                  _resident((n_kt, LANES, tk), lambda i, f: (0, (ROW_WIN + LANES) // LANES, 0)),
                  pl.BlockSpec((None, NSA_KV_GROUPS * n_blocks, tq), lambda i, f: (i, 0, 0)),
                  pl.BlockSpec((None, LANES, NSA_HEADS * tq), lambda i, f: (i, 0, 0)),
                  pl.BlockSpec((None, LANES, tq), lambda i, f: (i, 0, 0))],
        out_specs=pl.BlockSpec((tq, NSA_HEADS * LANES), lambda i, f: (i, 0)),
        scratch_shapes=[pltpu.VMEM((1, n_lanes), F32), pltpu.VMEM((1, n_lanes), F32), pltpu.VMEM((LANES, n_lanes), F32)])
    return pl.pallas_call(
        functools.partial(_nsa_attend_kernel, tq=tq, tk=tk, n_blocks=n_blocks), grid_spec=grid_spec,
        out_shape=jax.ShapeDtypeStruct((t, NSA_HEADS * LANES), BF16),
        compiler_params=_cparams(("arbitrary",)), name="nsa_attend",
    )(flags, qt, kn, kvt3, kn, kvt3, sel, ocmp, gt3)


def _page_specs(rows, n_pages, layer, reverse):
    def spec(j):
        def index(b, s, pt):
            slot = s * PAGES_PER_STEP + j
            if reverse:
                slot = n_pages - 1 - slot
            return (layer, pt[b, slot], 0, 0)
        return pl.BlockSpec((None, None, rows, PAGE_SIZE), index)
    return [spec(j) for j in range(PAGES_PER_STEP)]


def _page_suffix_sums(x, u, n_split):
    r = x.shape[0]
    n = x.shape[1] // PAGE_SIZE
    stacked = jnp.concatenate([x[:, j * PAGE_SIZE:(j + 1) * PAGE_SIZE] for j in range(n)], axis=0)
    y = _dot_split_lhs(stacked, u, n_split)
    tot = y[:, 0:1] + stacked[:, 0:1]
    inner = jnp.concatenate([y[j * r:(j + 1) * r] for j in range(n)], axis=1)
    return inner, [tot[j * r:(j + 1) * r] for j in range(n)]


def _page_offsets(carry, totals):
    r = carry.shape[0]
    cols = []
    for tot in totals:
        cols.append(jnp.broadcast_to(carry, (r, PAGE_SIZE)))
        carry = carry + tot
    return jnp.concatenate(cols, axis=1), carry


def _sb_pages_step(qbd, pages, u, carry_ref, acc_ref):
    kv = jnp.concatenate([page[...] for page in pages], axis=1).astype(BF16)
    z = jnp.dot(qbd, kv[0:256], preferred_element_type=F32)
    log_keep = -_softplus(z)
    inner, totals = _page_suffix_sums(log_keep, u, 2)
    offsets, carry = _page_offsets(carry_ref[...], totals)
    w = jnp.exp(z + log_keep + inner + offsets)
    acc_ref[...] += lax.dot_general(w.astype(BF16), kv[256:512], NT, preferred_element_type=F32)
    carry_ref[...] = carry


def _sb_sample_head_kernel(pt_ref, q_ref, new_ref, *rest, n_new):
    pages = rest[:PAGES_PER_STEP]
    acc_ref, carry_ref, live_ref = rest[PAGES_PER_STEP:]
    n_rows = SB_HEADS * n_new
    qbd = _stack_heads(q_ref[...], SB_HEADS)
    u = _strict_upper(PAGE_SIZE)
    carry_ref[...] = jnp.zeros(carry_ref.shape, F32)
    acc_ref[...] = jnp.zeros(acc_ref.shape, F32)
    new = new_ref[...].astype(BF16)
    valid = _iota((1, PAGE_SIZE), 1) < _row_t(n_rows, n_new)
    _sb_step(qbd, new[0:256], new[256:512], u, valid, carry_ref, acc_ref)
    _sb_pages_step(qbd, pages, u, carry_ref, acc_ref)
    live = jnp.max(carry_ref[...], axis=0, keepdims=True) > -SB_EXIT
    live_ref[...] = jnp.broadcast_to(live, live_ref.shape).astype(jnp.int32)


def _sb_sample_tail_kernel(pt_ref, live_ref, q_ref, accin_ref, carryin_ref, *rest, n_new):
    pages = rest[:PAGES_PER_STEP]
    o_ref, carry_ref, acc_ref = rest[PAGES_PER_STEP:]
    b, s = pl.program_id(0), pl.program_id(1)

    @pl.when(s == 0)
    def _():
        carry_ref[...] = carryin_ref[...]
        acc_ref[...] = accin_ref[...]

    @pl.when(jnp.logical_and(live_ref[b] > 0, jnp.max(carry_ref[...]) > -SB_EXIT))
    def _():
        _sb_pages_step(_stack_heads(q_ref[...], SB_HEADS), pages, _strict_upper(PAGE_SIZE), carry_ref, acc_ref)

    @pl.when(s == pl.num_programs(1) - 1)
    def _():
        o_ref[...] = _diag_select(acc_ref[...], SB_HEADS, n_new)


def _sb_sample(q, new_t, cache_t, page_table, layer, n_new):
    n_seq, n_pages = page_table.shape
    n_rows = SB_HEADS * n_new
    n_steps = n_pages // PAGES_PER_STEP
    assert n_steps >= 2
    caches = [cache_t] * PAGES_PER_STEP
    head_spec = pltpu.PrefetchScalarGridSpec(
        num_scalar_prefetch=1, grid=(n_seq, 1),
        in_specs=[pl.BlockSpec((n_new, 256), lambda b, s, pt: (b, 0)),
                  pl.BlockSpec((None, KV_ROWS_SB, PAGE_SIZE), lambda b, s, pt: (b, 0, 0))]
        + _page_specs(KV_ROWS_SB, n_pages, layer, True),
        out_specs=[pl.BlockSpec((n_rows, 256), lambda b, s, pt: (b, 0)),
                   pl.BlockSpec((n_rows, 1), lambda b, s, pt: (b, 0)),
                   pl.BlockSpec((None, 8, LANES), lambda b, s, pt: (b, 0, 0))])
    acc, carry, live = pl.pallas_call(
        functools.partial(_sb_sample_head_kernel, n_new=n_new), grid_spec=head_spec,
        out_shape=[jax.ShapeDtypeStruct((n_seq * n_rows, 256), F32), jax.ShapeDtypeStruct((n_seq * n_rows, 1), F32),
                   jax.ShapeDtypeStruct((n_seq, 8, LANES), jnp.int32)],
        compiler_params=_cparams(("arbitrary", "arbitrary")), name="sb_sample_head",
    )(page_table, q, new_t, *caches)

    def tail_page(j):
        def index(b, s, pt, lv):
            slot = n_pages - 1 - ((s + 1) * PAGES_PER_STEP + j)
            return (layer, jnp.where(lv[b] > 0, pt[b, slot], pt[0, 0]), 0, 0)
        return pl.BlockSpec((None, None, KV_ROWS_SB, PAGE_SIZE), index)

    tail_spec = pltpu.PrefetchScalarGridSpec(
        num_scalar_prefetch=2, grid=(n_seq, n_steps - 1),
        in_specs=[pl.BlockSpec((n_new, 256), lambda b, s, pt, lv: (b, 0)),
                  pl.BlockSpec((n_rows, 256), lambda b, s, pt, lv: (b, 0)),
                  pl.BlockSpec((n_rows, 1), lambda b, s, pt, lv: (b, 0))]
        + [tail_page(j) for j in range(PAGES_PER_STEP)],
        out_specs=pl.BlockSpec((n_new, 256), lambda b, s, pt, lv: (b, 0)),
        scratch_shapes=[pltpu.VMEM((n_rows, 1), F32), pltpu.VMEM((n_rows, 256), F32)])
    return pl.pallas_call(
        functools.partial(_sb_sample_tail_kernel, n_new=n_new), grid_spec=tail_spec,
        out_shape=jax.ShapeDtypeStruct((n_seq * n_new, 256), F32),
        compiler_params=_cparams(("arbitrary", "arbitrary")), name="sb_sample_tail",
    )(page_table, live[:, 0, 0], q, acc, carry, *caches)


def _expand_heads(lf, n_new):
    n = lf.shape[1]
    return jnp.concatenate([jnp.broadcast_to(lf[h:h + 1, :], (n_new, n)) for h in range(FOX_HEADS)], axis=0)


def _fox_sample_kernel(pt_ref, q_ref, new_ref, lfnew_ref, *rest, n_new):
    pages = rest[:PAGES_PER_STEP]
    lf_pages = rest[PAGES_PER_STEP:2 * PAGES_PER_STEP]
    o_ref, m_ref, l_ref, acc_ref, sfx_ref, cnew_ref = rest[2 * PAGES_PER_STEP:]
    s = pl.program_id(1)
    n_rows = FOX_HEADS * n_new
    qbd = _stack_heads(q_ref[...], FOX_HEADS)
    u = _strict_upper(PAGE_SIZE)
    lane = _iota((1, PAGE_SIZE), 1)
    t_row = _row_t(n_rows, n_new)

    @pl.when(s == 0)
    def _():
        _flash_init(m_ref, l_ref, acc_ref)
        sfx_ref[...] = jnp.zeros(sfx_ref.shape, F32)
        lf = _expand_heads(lfnew_ref[...], n_new)
        incl = jnp.where(_iota((PAGE_SIZE, PAGE_SIZE), 0) <= _iota((PAGE_SIZE, PAGE_SIZE), 1), 1.0, 0.0).astype(BF16)
        cum = _dot_split_lhs(lf, incl, 3)
        c_t = jnp.sum(jnp.where(lane == t_row, cum, 0.0), axis=1, keepdims=True)
        cnew_ref[...] = c_t
        new = new_ref[...].astype(BF16)
        sc = jnp.dot(qbd, new[0:256], preferred_element_type=F32) + (c_t - cum)
        sc = jnp.where(lane <= t_row, sc, NEG_INF)
        _flash_update(sc, new[256:512], m_ref, l_ref, acc_ref)

    kv = jnp.concatenate([page[...] for page in pages], axis=1).astype(BF16)
    lf = _expand_heads(jnp.concatenate([lf_page[...] for lf_page in lf_pages], axis=1), n_new)
    inner, totals = _page_suffix_sums(lf, u, 3)
    offsets, sfx = _page_offsets(sfx_ref[...], totals)
    sc = jnp.dot(qbd, kv[0:256], preferred_element_type=F32) + (inner + offsets + cnew_ref[...])
    _flash_update(sc, kv[256:512], m_ref, l_ref, acc_ref)
    sfx_ref[...] = sfx

    @pl.when(s == pl.num_programs(1) - 1)
    def _():
        o_ref[...] = _diag_select(_flash_out(l_ref, acc_ref), FOX_HEADS, n_new)


def _fox_sample(q, new_t, lf_new, cache_t, lf_cache_t, page_table, layer, n_new):
    n_seq, n_pages = page_table.shape
    n_rows = FOX_HEADS * n_new
    grid_spec = pltpu.PrefetchScalarGridSpec(
        num_scalar_prefetch=1, grid=(n_seq, n_pages // PAGES_PER_STEP),
        in_specs=[pl.BlockSpec((n_new, 256), lambda b, s, pt: (b, 0)),
                  pl.BlockSpec((None, KV_ROWS_FOX, PAGE_SIZE), lambda b, s, pt: (b, 0, 0)),
                  pl.BlockSpec((None, FOX_HEADS, PAGE_SIZE), lambda b, s, pt: (b, 0, 0))]
        + _page_specs(KV_ROWS_FOX, n_pages, layer, True) + _page_specs(FOX_HEADS, n_pages, layer, True),
        out_specs=pl.BlockSpec((n_new, 256), lambda b, s, pt: (b, 0)),
        scratch_shapes=[pltpu.VMEM((n_rows, 1), F32), pltpu.VMEM((n_rows, 1), F32), pltpu.VMEM((n_rows, 256), F32),
                        pltpu.VMEM((n_rows, 1), F32), pltpu.VMEM((n_rows, 1), F32)])
    return pl.pallas_call(
        functools.partial(_fox_sample_kernel, n_new=n_new), grid_spec=grid_spec,
        out_shape=jax.ShapeDtypeStruct((n_seq * n_new, 256), F32),
        compiler_params=_cparams(("arbitrary", "arbitrary")), name="fox_sample",
    )(page_table, q, new_t, lf_new, *([cache_t] * PAGES_PER_STEP), *([lf_cache_t] * PAGES_PER_STEP))


def _write_head_padded(o_all, n_new, o_ref):
    for h in range(NSA_HEADS):
        o_ref[:, h * LANES:(h + 1) * LANES] = o_all[h * n_new:(h + 1) * n_new]


def _cmp_sample_kernel(pt_ref, q_ref, cw_ref, *rest, n_new, past_len):
    pages = rest[:PAGES_PER_STEP]
    o_ref, sel_ref, need_ref, ckt_ref = rest[PAGES_PER_STEP:]
    s = pl.program_id(1)
    n_rows = NSA_HEADS * n_new
    n_chunks, _, chunk_lanes = ckt_ref.shape
    n_blocks = n_chunks * chunk_lanes
    n_keys = PAGES_PER_STEP * PAGE_SIZE
    blocks_per_step = n_keys // CMP_BLOCK
    steps_per_chunk = chunk_lanes // blocks_per_step
    blk_lane = _iota((1, n_blocks), 1)

    @pl.when(s == 0)
    def _():
        ckt_ref[...] = jnp.zeros(ckt_ref.shape, F32)

    keys = jnp.concatenate([page[...] for page in pages], axis=1).astype(BF16)
    col = (s % steps_per_chunk) * blocks_per_step + _iota((n_keys, chunk_lanes), 0) // CMP_BLOCK
    placed = jnp.where(_iota((n_keys, chunk_lanes), 1) == col, cw_ref[...], 0.0).astype(BF16)
    ckt_ref[s // steps_per_chunk] += jnp.dot(keys, placed, preferred_element_type=F32)

    @pl.when(s == pl.num_programs(1) - 1)
    def _():
        qbd = _stack_nsa(q_ref[...])
        slope = _nsa_slopes(n_rows, n_new)
        t_pos = past_len + _row_t(n_rows, n_new)
        cmp_end = blk_lane * CMP_BLOCK + (CMP_BLOCK - 1)
        ck = jnp.concatenate([ckt_ref[c] for c in range(n_chunks)], axis=1)
        s_c = jnp.dot(qbd, ck[0:128].astype(BF16), preferred_element_type=F32)
        s_c = s_c - slope * (t_pos - cmp_end).astype(F32)
        p_c = _masked_softmax_rows(s_c, cmp_end <= t_pos)
        o_cmp = lax.dot_general(p_c.astype(BF16), ck[128:256].astype(BF16), NT, preferred_element_type=F32)
        _write_head_padded(o_cmp, n_new, o_ref)
        cur = (past_len + _iota((n_new, 1), 0)) // SEL_BLOCK
        blk = _iota((n_new, n_blocks), 1)
        forced = jnp.logical_or(blk == cur - 1, blk == 0)
        for g in range(NSA_KV_GROUPS):
            score = sum(p_c[(g * NSA_GROUP_SIZE + r) * n_new:(g * NSA_GROUP_SIZE + r + 1) * n_new]
                        for r in range(NSA_GROUP_SIZE))
            score = jnp.where(forced, FORCE_SCORE, jnp.where(blk > cur, -FORCE_SCORE, score))
            sel_ref[g * n_new:(g + 1) * n_new, :] = _top_blocks(score, min(SEL_TOPK - 1, n_blocks))
        n_pages = need_ref.shape[1]
        in_page = jnp.where(_iota((n_blocks, n_pages), 0) // (PAGE_SIZE // SEL_BLOCK) == _iota((n_blocks, n_pages), 1),
                            1.0, 0.0).astype(BF16)
        hits = jnp.dot(sel_ref[...].astype(BF16), in_page, preferred_element_type=F32)
        need_ref[...] = (jnp.max(hits, axis=0, keepdims=True) > 0.5).astype(jnp.int32)


def _cmp_sample(nsaq, cw_keys, cache_t, page_table, layer, n_new, past_len):
    n_seq, n_pages = page_table.shape
    n_blocks = past_len // CMP_BLOCK
    n_keys, chunk_lanes = cw_keys.shape
    assert n_blocks % chunk_lanes == 0 and chunk_lanes % (n_keys // CMP_BLOCK) == 0
    grid_spec = pltpu.PrefetchScalarGridSpec(
        num_scalar_prefetch=1, grid=(n_seq, n_pages // PAGES_PER_STEP),
        in_specs=[pl.BlockSpec((n_new, 1024), lambda b, s, pt: (b, 0)),
                  pl.BlockSpec((n_keys, chunk_lanes), lambda b, s, pt: (0, 0))]
        + _page_specs(KV_ROWS_NSA, n_pages, layer, False),
        out_specs=[pl.BlockSpec((n_new, 1024), lambda b, s, pt: (b, 0)),
                   pl.BlockSpec((None, NSA_KV_GROUPS * n_new, n_blocks), lambda b, s, pt: (b, 0, 0)),
                   pl.BlockSpec((None, 1, n_pages), lambda b, s, pt: (b, 0, 0))],
        scratch_shapes=[pltpu.VMEM((n_blocks // chunk_lanes, KV_ROWS_NSA, chunk_lanes), F32)])
    return pl.pallas_call(
        functools.partial(_cmp_sample_kernel, n_new=n_new, past_len=past_len), grid_spec=grid_spec,
        out_shape=[jax.ShapeDtypeStruct((n_seq * n_new, 1024), F32),
                   jax.ShapeDtypeStruct((n_seq, NSA_KV_GROUPS * n_new, n_blocks), F32),
                   jax.ShapeDtypeStruct((n_seq, 1, n_pages), jnp.int32)],
        compiler_params=_cparams(("arbitrary", "arbitrary")), name="cmp_sample",
    )(page_table, nsaq, cw_keys, *([cache_t] * PAGES_PER_STEP))


def _sel_sample_kernel(pt_ref, need_ref, q_ref, new_ref, sel_ref, *rest, n_new, past_len):
    pages = rest[:PAGES_PER_STEP]
    o_ref, m_ref, l_ref, acc_ref = rest[PAGES_PER_STEP:]
    b, s = pl.program_id(0), pl.program_id(1)
    n_rows = NSA_HEADS * n_new
    n_blocks = sel_ref.shape[1]
    qbd = _stack_nsa(q_ref[...])
    slope = _nsa_slopes(n_rows, n_new)
    t_row = _row_t(n_rows, n_new)
    lane = _iota((1, PAGE_SIZE), 1)
    blocks_per_page = PAGE_SIZE // SEL_BLOCK

    @pl.when(s == 0)
    def _():
        _flash_init(m_ref, l_ref, acc_ref)
        new = new_ref[...].astype(BF16)
        dist = (t_row - lane).astype(F32)
        sc = jnp.dot(qbd, new[0:128], preferred_element_type=F32) - slope * dist
        _flash_update(jnp.where(dist >= 0.0, sc, NEG_INF), new[128:256], m_ref, l_ref, acc_ref)

    needed = sum(need_ref[b, s * PAGES_PER_STEP + j] for j in range(PAGES_PER_STEP))

    @pl.when(needed > 0)
    def _():
        sel = sel_ref[...].astype(BF16)
        n_keys = PAGES_PER_STEP * PAGE_SIZE
        kv = jnp.concatenate([page[...] for page in pages], axis=1).astype(BF16)
        expand = jnp.where(_iota((n_blocks, n_keys), 0)
                           == s * (PAGES_PER_STEP * blocks_per_page) + _iota((n_blocks, n_keys), 1) // SEL_BLOCK,
                           1.0, 0.0).astype(BF16)
        ch = jnp.dot(sel, expand, preferred_element_type=F32)
        chosen = _group_rows(ch[0:n_new], ch[n_new:2 * n_new])
        dist = (past_len + t_row - (s * n_keys + _iota((1, n_keys), 1))).astype(F32)
        sc = jnp.dot(qbd, kv[0:128], preferred_element_type=F32) - slope * dist
        _flash_update(jnp.where(chosen > 0.5, sc, NEG_INF), kv[128:256], m_ref, l_ref, acc_ref)

    @pl.when(s == pl.num_programs(1) - 1)
    def _():
        _write_head_padded(_flash_out(l_ref, acc_ref), n_new, o_ref)


def _sel_sample(nsaq, new_t, sel, need, cache_t, page_table, layer, n_new, past_len):
    n_seq, n_pages = page_table.shape
    n_rows = NSA_HEADS * n_new
    n_blocks = sel.shape[2]

    def page(j):
        def index(b, s, pt, nd):
            slot = s * PAGES_PER_STEP + j
            return (layer, jnp.where(nd[b, slot] > 0, pt[b, slot], pt[0, 0]), 0, 0)
        return pl.BlockSpec((None, None, KV_ROWS_NSA, PAGE_SIZE), index)

    grid_spec = pltpu.PrefetchScalarGridSpec(
        num_scalar_prefetch=2, grid=(n_seq, n_pages // PAGES_PER_STEP),
        in_specs=[pl.BlockSpec((n_new, 1024), lambda b, s, pt, nd: (b, 0)),
                  pl.BlockSpec((None, KV_ROWS_NSA, PAGE_SIZE), lambda b, s, pt, nd: (b, 0, 0)),
                  pl.BlockSpec((None, NSA_KV_GROUPS * n_new, n_blocks), lambda b, s, pt, nd: (b, 0, 0))]
        + [page(j) for j in range(PAGES_PER_STEP)],
        out_specs=pl.BlockSpec((n_new, 1024), lambda b, s, pt, nd: (b, 0)),
        scratch_shapes=[pltpu.VMEM((n_rows, 1), F32), pltpu.VMEM((n_rows, 1), F32), pltpu.VMEM((n_rows, LANES), F32)])
    return pl.pallas_call(
        functools.partial(_sel_sample_kernel, n_new=n_new, past_len=past_len), grid_spec=grid_spec,
        out_shape=jax.ShapeDtypeStruct((n_seq * n_new, 1024), F32),
        compiler_params=_cparams(("arbitrary", "arbitrary")), name="sel_sample",
    )(page_table, need, nsaq, new_t, sel, *([cache_t] * PAGES_PER_STEP))


def _win_sample_kernel(q_ref, new_ref, win_ref, ocmp_ref, osel_ref, gate_ref, o_ref, *, n_new):
    n_rows = NSA_HEADS * n_new
    wb = win_ref.shape[1]
    qbd = _stack_nsa(q_ref[...])
    slope = _nsa_slopes(n_rows, n_new)
    t_row = _row_t(n_rows, n_new)
    kv = jnp.concatenate([win_ref[...], new_ref[...]], axis=1).astype(BF16)
    dist = (wb + t_row - _iota((1, wb + PAGE_SIZE), 1)).astype(F32)
    sc = jnp.dot(qbd, kv[0:128], preferred_element_type=F32) - slope * dist
    valid = jnp.logical_and(dist >= 0.0, dist < float(WINDOW))
    p = _masked_softmax_rows(sc, valid)
    o_win = lax.dot_general(p.astype(BF16), kv[128:256], NT, preferred_element_type=F32)
    gates = gate_ref[...]
    for h in range(NSA_HEADS):
        cols = slice(h * LANES, (h + 1) * LANES)
        o_ref[:, cols] = (gates[:, GATE_LANE + h:GATE_LANE + h + 1] * ocmp_ref[:, cols]
                          + gates[:, GATE_LANE + NSA_HEADS + h:GATE_LANE + NSA_HEADS + h + 1] * osel_ref[:, cols]
                          + gates[:, GATE_LANE + 2 * NSA_HEADS + h:GATE_LANE + 2 * NSA_HEADS + h + 1]
                          * o_win[h * n_new:(h + 1) * n_new])


def _win_sample(nsaq, new_t, win_t, o_cmp, o_sel, small, n_new):
    n_seq, _, wb = win_t.shape
    row = lambda b: (b, 0)
    return pl.pallas_call(
        functools.partial(_win_sample_kernel, n_new=n_new),
        grid=(n_seq,),
        in_specs=[pl.BlockSpec((n_new, 1024), row),
                  pl.BlockSpec((None, KV_ROWS_NSA, PAGE_SIZE), lambda b: (b, 0, 0)),
                  pl.BlockSpec((None, KV_ROWS_NSA, wb), lambda b: (b, 0, 0)),
                  pl.BlockSpec((n_new, 1024), row), pl.BlockSpec((n_new, 1024), row),
                  pl.BlockSpec((n_new, LANES), row)],
        out_specs=pl.BlockSpec((n_new, 1024), row),
        out_shape=jax.ShapeDtypeStruct((n_seq * n_new, 1024), F32),
        compiler_params=_cparams(("arbitrary",)), name="win_sample",
    )(nsaq, new_t, win_t, o_cmp, o_sel, small)


def _layer_weights(l, w_in, b_forget, cmp_w, w_br_sb, w_br_nsa, w_br_fox, w_out, tm):
    w = w_in[l]
    scale = HEAD_DIM ** -0.5
    zeros64 = jnp.zeros((D_MODEL, HEAD_DIM), F32)
    nsa_q = w[:, 768:1280] * scale
    pads = []
    for h in range(NSA_HEADS):
        qh = nsa_q[:, h * HEAD_DIM:(h + 1) * HEAD_DIM]
        pads += [qh, zeros64] if h < NSA_GROUP_SIZE else [zeros64, qh]
    wq = jnp.concatenate([w[:, 0:256] * scale] + pads + [w[:, 2072:2328] * scale], axis=1)
    wkv = jnp.concatenate([w[:, 256:768], w[:, 2328:2840], w[:, 1280:1536], w[:, 1536:1792], w[:, 1792:2048]], axis=1)
    wsm = jnp.concatenate([w[:, 2048:2072], w[:, 2840:2844], jnp.zeros((D_MODEL, LANES - 28), F32)], axis=1)
    brow = jnp.zeros((1, LANES), F32).at[0, LOGF_LANE:LOGF_LANE + FOX_HEADS].set(b_forget[l])
    wn = w_br_nsa[l].reshape(NSA_HEADS, HEAD_DIM, D_MODEL)
    zrow = jnp.zeros((HEAD_DIM, D_MODEL), F32)
    rows = []
    for h in range(NSA_HEADS):
        rows += [wn[h], zrow] if h < NSA_GROUP_SIZE else [zrow, wn[h]]
    wkn = jnp.concatenate([w[:, 2328:2584], w[:, 1536:1664], w[:, 1792:1920]], axis=1)
    return dict(
        wq=wq.astype(BF16), wkvt=wkv.T.astype(BF16), wqt=wq[:, 256:1536].T.astype(BF16), wkn=wkn.astype(BF16),
        wcmp=w[:, 1280:1536].astype(BF16), wsm=wsm.astype(BF16),
        brow=brow, cw=jnp.broadcast_to(jnp.tile(cmp_w[l], tm // CMP_BLOCK)[:, None], (tm, KV_ROWS_NSA)),
        wgate=w[:, 2844:5916].astype(BF16), wbr_sb=w_br_sb[l].astype(BF16),
        wbr_nsa=jnp.concatenate(rows, axis=0).astype(BF16), wbr_fox=w_br_fox[l].astype(BF16),
        wout=w_out[l].astype(BF16))


def _cache_t(cache):
    d, n_pool, page = cache.shape[:3]
    return jnp.transpose(cache, (0, 1, 3, 4, 5, 2)).reshape(d, n_pool, -1, page)


def _new_pages(kvt_rows, n_seq, n_new):
    r = kvt_rows.shape[0]
    x = jnp.transpose(kvt_rows.reshape(r, n_seq, n_new), (1, 0, 2))
    return jnp.pad(x, ((0, 0), (0, 0), (0, PAGE_SIZE - n_new)))


def _kv_out(kvt_rows, lead_shape, heads):
    n = kvt_rows.shape[1]
    return jnp.transpose(kvt_rows.reshape(2, heads, HEAD_DIM, n), (3, 0, 1, 2)).reshape(*lead_shape, 2, heads, HEAD_DIM)


def kernel(x_prompt, x_sample, cache_sb_kv, cache_cmp_kv, cache_sel_kv, cache_fox_kv, cache_fox_logf, state_win_kv, page_table, ffn1_pre_g, ffn1_post_g, ffn1_w_gu, ffn1_w_down, mix_pre_g, mix_post_g, w_in, b_forget, cmp_w, w_br_sb, w_br_nsa, w_br_fox, w_out, ffn2_pre_g, ffn2_post_g, ffn2_w_gu, ffn2_w_down):
    depth = w_in.shape[0]
    batch, seq, _ = x_prompt.shape
    n_seq, n_new, _ = x_sample.shape
    past_len = page_table.shape[1] * PAGE_SIZE
    assert batch == 1 and seq % ATT_TILE == 0 and seq >= WINDOW + ATT_TILE
    assert past_len % SEL_BLOCK == 0 and n_new <= CMP_BLOCK and page_table.shape[1] % PAGES_PER_STEP == 0
    m_s = n_seq * n_new
    tm_s = m_s if m_s <= ATT_TILE else ATT_TILE
    assert m_s % tm_s == 0 and tm_s % CMP_BLOCK == 0

    x_p = x_prompt.reshape(seq, D_MODEL)
    x_s = x_sample.reshape(m_s, D_MODEL)
    sb_c, cmp_c, sel_c, fox_c = (_cache_t(c) for c in (cache_sb_kv, cache_cmp_kv, cache_sel_kv, cache_fox_kv))
    lf_c = jnp.transpose(cache_fox_logf, (0, 1, 3, 2))
    wb = state_win_kv.shape[2]
    win_c = jnp.transpose(state_win_kv, (0, 1, 3, 4, 5, 2)).reshape(depth, n_seq, KV_ROWS_NSA, wb)
    row = lambda a, l: a[l][None, :]
    outs = {k: [] for k in ("sb_p", "sb_s", "cmp_p", "cmp_s", "sel_p", "sel_s", "fox_p", "fox_s", "lf_p", "lf_s",
                            "win_p", "win_s")}
    ffn_tm_p = 512 if seq % 512 == 0 else ATT_TILE
    for l in range(depth):
        lw = _layer_weights(l, w_in, b_forget, cmp_w, w_br_sb, w_br_nsa, w_br_fox, w_out, ATT_TILE)
        lw_s = dict(lw, cw=lw["cw"][:tm_s])
        wgu1, wd1 = ffn1_w_gu[l].astype(BF16), ffn1_w_down[l].astype(BF16)
        wgu2, wd2 = ffn2_w_gu[l].astype(BF16), ffn2_w_down[l].astype(BF16)
        x_p = _ffn(x_p, row(ffn1_pre_g, l), row(ffn1_post_g, l), wgu1, wd1, ffn_tm_p)
        x_s = _ffn(x_s, row(ffn1_pre_g, l), row(ffn1_post_g, l), wgu1, wd1, tm_s)

        pp = _proj(x_p, row(mix_pre_g, l), lw, ATT_TILE)
        ck = pp["ck"].reshape(seq // CMP_BLOCK, KV_ROWS_NSA)
        o_sb = _sb_prompt(pp["sbq"], pp["kvt3"], Q_TILE)
        o_fox = _fox_prompt(pp["qt"], pp["kn"], pp["kvt3"], pp["fcum"], pp["kmax"], Q_TILE, FOX_KEYS)
        o_cmp, sel_blocks, flags = _nsa_select(pp["qt"], ck, jnp.transpose(ck), Q_TILE, ATT_TILE)
        o_nsa = _nsa_attend(flags.reshape(-1), pp["qt"], pp["kn"], pp["kvt3"], sel_blocks, o_cmp, pp["gt3"], Q_TILE)
        x_p = _merge(x_p, o_sb, o_nsa, o_fox, row(mix_pre_g, l), row(mix_post_g, l), lw, ATT_TILE)

        ps = _proj(x_s, row(mix_pre_g, l), lw_s, tm_s)
        kvt_s = ps["kvt"]
        new = lambda r0, n: _new_pages(kvt_s[r0:r0 + n], n_seq, n_new)
        lf_new = _new_pages(jnp.transpose(ps["small"][:, LOGF_LANE:LOGF_LANE + FOX_HEADS]), n_seq, n_new)
        o_sb = _sb_sample(ps["sbq"], new(ROW_SB, KV_ROWS_SB), sb_c, page_table, l, n_new)
        o_fox = _fox_sample(ps["foxq"], new(ROW_FOX, KV_ROWS_FOX), lf_new, fox_c, lf_c, page_table, l, n_new)
        step_keys = PAGES_PER_STEP * PAGE_SIZE
        cw_keys = jnp.broadcast_to(jnp.tile(cmp_w[l], step_keys // CMP_BLOCK)[:, None],
                                   (step_keys, min(LANES, past_len // CMP_BLOCK)))
        o_cmp, sel, need = _cmp_sample(ps["nsaq"], cw_keys, cmp_c, page_table, l, n_new, past_len)
        o_sel = _sel_sample(ps["nsaq"], new(ROW_SEL, KV_ROWS_NSA), sel, need[:, 0, :], sel_c, page_table, l, n_new,
                            past_len)
        new_win = new(ROW_WIN, KV_ROWS_NSA)
        o_nsa = _win_sample(ps["nsaq"], new_win, win_c[l], o_cmp, o_sel, ps["small"], n_new)
        x_s = _merge(x_s, o_sb, o_nsa, o_fox, row(mix_pre_g, l), row(mix_post_g, l), lw, tm_s)

        x_p = _ffn(x_p, row(ffn2_pre_g, l), row(ffn2_post_g, l), wgu2, wd2, ffn_tm_p)
        x_s = _ffn(x_s, row(ffn2_pre_g, l), row(ffn2_post_g, l), wgu2, wd2, tm_s)

        kvt_p = pp["kvt"]
        outs["sb_p"].append(_kv_out(kvt_p[ROW_SB:ROW_SB + 512], (batch, seq), SB_HEADS))
        outs["sb_s"].append(_kv_out(kvt_s[ROW_SB:ROW_SB + 512], (n_seq, n_new), SB_HEADS))
        outs["cmp_p"].append(_kv_out(kvt_p[ROW_CMP:ROW_CMP + 256], (batch, seq), NSA_KV_GROUPS))
        outs["cmp_s"].append(_kv_out(kvt_s[ROW_CMP:ROW_CMP + 256], (n_seq, n_new), NSA_KV_GROUPS))
        outs["sel_p"].append(_kv_out(kvt_p[ROW_SEL:ROW_SEL + 256], (batch, seq), NSA_KV_GROUPS))
        outs["sel_s"].append(_kv_out(kvt_s[ROW_SEL:ROW_SEL + 256], (n_seq, n_new), NSA_KV_GROUPS))
        outs["fox_p"].append(_kv_out(kvt_p[ROW_FOX:ROW_FOX + 512], (batch, seq), FOX_HEADS))
        outs["fox_s"].append(_kv_out(kvt_s[ROW_FOX:ROW_FOX + 512], (n_seq, n_new), FOX_HEADS))
        outs["lf_p"].append(pp["small"][:, LOGF_LANE:LOGF_LANE + FOX_HEADS].reshape(batch, seq, FOX_HEADS))
        outs["lf_s"].append(ps["small"][:, LOGF_LANE:LOGF_LANE + FOX_HEADS].reshape(n_seq, n_new, FOX_HEADS))
        w_keep = min(WINDOW, seq)
        outs["win_p"].append(_kv_out(kvt_p[ROW_WIN:ROW_WIN + 256, seq - w_keep:], (batch, w_keep), NSA_KV_GROUPS))
        win_all = jnp.concatenate([win_c[l], new_win[:, :, :n_new]], axis=2)[:, :, n_new:]
        outs["win_s"].append(jnp.transpose(win_all.reshape(n_seq, 2, NSA_KV_GROUPS, HEAD_DIM, wb), (0, 4, 1, 2, 3)))

    st = lambda k: jnp.stack(outs[k])
    return (x_p.reshape(batch, seq, D_MODEL), x_s.reshape(n_seq, n_new, D_MODEL),
            st("sb_p"), st("sb_s"), st("cmp_p"), st("cmp_s"), st("sel_p"), st("sel_s"),
            st("fox_p"), st("fox_s"), st("lf_p"), st("lf_s"), st("win_p"), st("win_s"))
```

```python
import functools

import jax
import jax.numpy as jnp
from jax import lax
from jax.experimental import pallas as pl
from jax.experimental.pallas import tpu as pltpu

F32 = jnp.float32
BF16 = jnp.bfloat16

D_MODEL = 1024
HEAD_DIM = 64
SB_HEADS = 4
NSA_HEADS = 8
NSA_KV_GROUPS = 2
NSA_GROUP_SIZE = NSA_HEADS // NSA_KV_GROUPS
FOX_HEADS = 4
CMP_BLOCK = 64
SEL_BLOCK = 64
SEL_TOPK = 16
WINDOW = 512
D_FF = 2816
PAGE_SIZE = 128
RMS_EPS = 1e-6
NEG_INF = -1e30
FORCE_SCORE = 1e4

LANES = 128
V7X_VMEM_LIMIT_BYTES = 56 * 1024 * 1024
ATT_TILE = 256
Q_TILE = 256
FOX_KEYS = 512
FFN_CHUNK = 1408
PAGES_PER_STEP = 32
SB_EXIT = 110.0

KV_ROWS_SB = 2 * SB_HEADS * HEAD_DIM
KV_ROWS_FOX = 2 * FOX_HEADS * HEAD_DIM
KV_ROWS_NSA = 2 * NSA_KV_GROUPS * HEAD_DIM
ROW_SB, ROW_FOX, ROW_CMP, ROW_SEL, ROW_WIN = 0, 512, 1024, 1280, 1536
KV_ROWS_ALL = 1792
QT_ROWS = 1280
KN_COLS = 512
EXP_ZERO = 110.0
GATE_LANE = 0
LOGF_LANE = 24

NT = (((1,), (1,)), ((), ()))


def _cparams(sem):
    return pltpu.CompilerParams(dimension_semantics=sem, vmem_limit_bytes=V7X_VMEM_LIMIT_BYTES)


def _rms(x, g):
    return x * lax.rsqrt(jnp.mean(x * x, axis=-1, keepdims=True) + RMS_EPS) * g


def _softplus(z):
    return jnp.maximum(z, 0.0) + jnp.log(1.0 + jnp.exp(-jnp.abs(z)))


def _sigmoid(z):
    return 1.0 / (1.0 + jnp.exp(-z))


def _split_bf16(x, n):
    parts = []
    r = x
    for _ in range(n):
        p = r.astype(BF16)
        parts.append(p)
        r = r - p.astype(F32)
    return parts


def _dot_split_lhs(x, u, n):
    return sum(jnp.dot(p, u, preferred_element_type=F32) for p in _split_bf16(x, n))


def _dot_split_rhs(u, x, n):
    return sum(jnp.dot(u, p, preferred_element_type=F32) for p in _split_bf16(x, n))


def _iota(shape, dim):
    return lax.broadcasted_iota(jnp.int32, shape, dim)


def _stack_heads(q, n_heads):
    lane_head = _iota(q.shape, 1) // HEAD_DIM
    return jnp.concatenate([jnp.where(lane_head == h, q, 0.0) for h in range(n_heads)], axis=0).astype(BF16)


def _stack_nsa(qpad):
    return jnp.concatenate([qpad[:, h * LANES:(h + 1) * LANES] for h in range(NSA_HEADS)], axis=0).astype(BF16)


def _diag_select(o_all, n_heads, tq):
    lane_head = _iota((tq, o_all.shape[1]), 1) // HEAD_DIM
    out = o_all[0:tq]
    for h in range(1, n_heads):
        out = jnp.where(lane_head == h, o_all[h * tq:(h + 1) * tq], out)
    return out


def _row_t(n_rows, tq):
    return _iota((n_rows, 1), 0) % tq


def _nsa_slopes(n_rows, tq):
    head = _iota((n_rows, 1), 0) // tq
    slope = jnp.full((n_rows, 1), 0.5, F32)
    for h in range(1, NSA_HEADS):
        slope = jnp.where(head == h, 2.0 ** -(h + 1), slope)
    return slope


def _flash_update(s, v_t, m_ref, l_ref, acc_ref):
    m_old = m_ref[...]
    m_new = jnp.maximum(m_old, jnp.max(s, axis=1, keepdims=True))
    alpha = jnp.exp(m_old - m_new)
    p = jnp.exp(s - m_new)
    l_ref[...] = alpha * l_ref[...] + jnp.sum(p, axis=1, keepdims=True)
    acc_ref[...] = alpha * acc_ref[...] + lax.dot_general(p.astype(BF16), v_t, NT, preferred_element_type=F32)
    m_ref[...] = m_new


def _flash_init(m_ref, l_ref, acc_ref):
    m_ref[...] = jnp.full(m_ref.shape, NEG_INF, F32)
    l_ref[...] = jnp.zeros(l_ref.shape, F32)
    acc_ref[...] = jnp.zeros(acc_ref.shape, F32)


def _flash_out(l_ref, acc_ref):
    return acc_ref[...] / jnp.maximum(l_ref[...], 1e-30)


def _flash_update_t(s, v_t, m_ref, l_ref, acc_ref):
    m_old = m_ref[...]
    m_new = jnp.maximum(m_old, jnp.max(s, axis=0, keepdims=True))
    alpha = jnp.exp(m_old - m_new)
    p = jnp.exp(s - m_new)
    l_ref[...] = alpha * l_ref[...] + jnp.sum(p, axis=0, keepdims=True)
    acc_ref[...] = alpha * acc_ref[...] + jnp.dot(v_t, p.astype(BF16), preferred_element_type=F32)
    m_ref[...] = m_new


def _top_blocks_t(score, n_pick):
    nb = score.shape[0]
    blk = _iota(score.shape, 0).astype(F32)
    sel = jnp.zeros(score.shape, F32)
    for _ in range(n_pick):
        m = jnp.max(score, axis=0, keepdims=True)
        first = jnp.min(jnp.where(score == m, blk, float(nb)), axis=0, keepdims=True)
        hit = blk == first
        sel = jnp.where(hit, 1.0, sel)
        score = jnp.where(hit, -jnp.inf, score)
    return sel


def _masked_softmax_rows(s, valid):
    s = jnp.where(valid, s, NEG_INF)
    m = jnp.max(s, axis=1, keepdims=True)
    p = jnp.where(valid, jnp.exp(s - m), 0.0)
    return p / jnp.maximum(jnp.sum(p, axis=1, keepdims=True), 1e-30)


def _top_blocks(score, n_pick):
    nb = score.shape[1]
    blk = _iota(score.shape, 1).astype(F32)
    sel = jnp.zeros(score.shape, F32)
    for _ in range(n_pick):
        m = jnp.max(score, axis=1, keepdims=True)
        first = jnp.min(jnp.where(score == m, blk, float(nb)), axis=1, keepdims=True)
        hit = blk == first
        sel = jnp.where(hit, 1.0, sel)
        score = jnp.where(hit, -jnp.inf, score)
    return sel


def _sb_step(qbd, k_t, v_t, u, valid, carry_ref, acc_ref):
    z = jnp.dot(qbd, k_t, preferred_element_type=F32)
    log_keep = -_softplus(z)
    if valid is not None:
        log_keep = jnp.where(valid, log_keep, 0.0)
    after = _dot_split_lhs(log_keep, u, 2)
    carry = carry_ref[...]
    w = jnp.exp(z + log_keep + after + carry)
    if valid is not None:
        w = jnp.where(valid, w, 0.0)
    acc_ref[...] += lax.dot_general(w.astype(BF16), v_t, NT, preferred_element_type=F32)
    carry_ref[...] = carry + after[:, 0:1] + log_keep[:, 0:1]


def _strict_upper(n):
    return jnp.where(_iota((n, n), 0) > _iota((n, n), 1), 1.0, 0.0).astype(BF16)


def _ffn_kernel(x_ref, pre_ref, post_ref, wgu_ref, wd_ref, o_ref):
    x = x_ref[...]
    h = _rms(x, pre_ref[...]).astype(BF16)
    acc = jnp.zeros(x.shape, F32)
    for c in range(D_FF // FFN_CHUNK):
        lo = c * FFN_CHUNK
        gate = jnp.dot(h, wgu_ref[:, lo:lo + FFN_CHUNK], preferred_element_type=F32)
        up = jnp.dot(h, wgu_ref[:, D_FF + lo:D_FF + lo + FFN_CHUNK], preferred_element_type=F32)
        a = (gate * _sigmoid(gate) * up).astype(BF16)
        acc = acc + jnp.dot(a, wd_ref[lo:lo + FFN_CHUNK, :], preferred_element_type=F32)
    o_ref[...] = x + 0.5 * _rms(acc, post_ref[...])


def _ffn(x, pre_g, post_g, w_gu, w_down, tm):
    m = x.shape[0]
    const = lambda i: (0, 0)
    return pl.pallas_call(
        _ffn_kernel,
        grid=(m // tm,),
        in_specs=[pl.BlockSpec((tm, D_MODEL), lambda i: (i, 0)),
                  pl.BlockSpec((1, D_MODEL), const), pl.BlockSpec((1, D_MODEL), const),
                  pl.BlockSpec((D_MODEL, 2 * D_FF), const, pipeline_mode=pl.Buffered(1)),
                  pl.BlockSpec((D_FF, D_MODEL), const, pipeline_mode=pl.Buffered(1))],
        out_specs=pl.BlockSpec((tm, D_MODEL), lambda i: (i, 0)),
        out_shape=jax.ShapeDtypeStruct((m, D_MODEL), F32),
        compiler_params=_cparams(("arbitrary",)),
        name="half_ffn",
    )(x, pre_g, post_g, w_gu, w_down)


def _proj_kernel(x_ref, g_ref, wq_ref, wkvt_ref, wqt_ref, wkn_ref, wcmp_ref, wsm_ref, brow_ref, cw_ref,
                 sbq_ref, nsaq_ref, foxq_ref, kvt_ref, kvt3_ref, qt_ref, kn_ref, ck_ref, small_ref, fcum_ref, gt3_ref,
                 kmax_ref, carry_ref, kmx_ref, *, tm):
    @pl.when(pl.program_id(0) == 0)
    def _():
        carry_ref[...] = jnp.zeros(carry_ref.shape, F32)
        kmx_ref[...] = jnp.zeros(kmx_ref.shape, F32)

    h = _rms(x_ref[...], g_ref[...]).astype(BF16)
    q = jnp.dot(h, wq_ref[...], preferred_element_type=F32)
    sbq_ref[...] = q[:, 0:256]
    nsaq_ref[...] = q[:, 256:1280]
    foxq_ref[...] = q[:, 1280:1536]
    kvt = lax.dot_general(wkvt_ref[...], h, NT, preferred_element_type=F32)
    kvt_ref[...] = kvt
    kvt3_ref[...] = kvt.astype(BF16)
    qt_ref[...] = lax.dot_general(wqt_ref[...], h, NT, preferred_element_type=F32).astype(BF16)
    kn = jnp.dot(h, wkn_ref[...], preferred_element_type=F32)
    kn_ref[...] = kn.astype(BF16)
    kf = kn[:, 0:256]
    head_of = jnp.where(_iota((256, LANES), 0) // HEAD_DIM == _iota((256, LANES), 1), 1.0, 0.0).astype(BF16)
    norm2 = jnp.max(jnp.dot((kf * kf).astype(BF16), head_of, preferred_element_type=F32), axis=0, keepdims=True)
    kmx_ref[...] = jnp.maximum(kmx_ref[...], norm2)
    kmax_ref[...] = kmx_ref[...]
    ycmp = jnp.dot(h, wcmp_ref[...], preferred_element_type=F32)
    ck_ref[...] = (ycmp * cw_ref[...]).reshape(tm // CMP_BLOCK, CMP_BLOCK, KV_ROWS_NSA).sum(axis=1)
    ys = jnp.dot(h, wsm_ref[...], preferred_element_type=F32)
    lane = _iota((1, LANES), 1)
    log_f = -_softplus(-(ys + brow_ref[...]))
    is_f = jnp.logical_and(lane >= LOGF_LANE, lane < LOGF_LANE + FOX_HEADS)
    small = jnp.where(lane < LOGF_LANE, _sigmoid(ys), jnp.where(is_f, log_f, 0.0))
    small_ref[...] = small
    gq = gt3_ref.shape[2]
    for j in range(tm // gq):
        gt3_ref[j] = small[j * gq:(j + 1) * gq].T
    tri = jnp.where(_iota((tm, tm), 0) >= _iota((tm, tm), 1), 1.0, 0.0).astype(BF16)
    cum = _dot_split_rhs(tri, jnp.where(is_f, log_f, 0.0), 3) + carry_ref[...]
    carry_ref[...] = cum[tm - 1:tm, :]
    fcum_ref[...] = cum


def _proj(x, g, lw, tm):
    m = x.shape[0]
    n = m // tm
    const = lambda i: (0, 0)
    row = lambda i: (i, 0)
    gq = min(Q_TILE, tm)
    outs = pl.pallas_call(
        functools.partial(_proj_kernel, tm=tm),
        grid=(n,),
        in_specs=[pl.BlockSpec((tm, D_MODEL), row), pl.BlockSpec((1, D_MODEL), const),
                  pl.BlockSpec((D_MODEL, 1536), const), pl.BlockSpec((KV_ROWS_ALL, D_MODEL), const),
                  pl.BlockSpec((QT_ROWS, D_MODEL), const), pl.BlockSpec((D_MODEL, KN_COLS), const),
                  pl.BlockSpec((D_MODEL, KV_ROWS_NSA), const), pl.BlockSpec((D_MODEL, LANES), const),
                  pl.BlockSpec((1, LANES), const), pl.BlockSpec((tm, KV_ROWS_NSA), const)],
        out_specs=[pl.BlockSpec((tm, 256), row), pl.BlockSpec((tm, 1024), row), pl.BlockSpec((tm, 256), row),
                   pl.BlockSpec((KV_ROWS_ALL, tm), lambda i: (0, i)),
                   pl.BlockSpec((None, KV_ROWS_ALL, tm), lambda i: (i, 0, 0)),
                   pl.BlockSpec((QT_ROWS, tm), lambda i: (0, i)),
                   pl.BlockSpec((tm, KN_COLS), row),
                   pl.BlockSpec((None, tm // CMP_BLOCK, KV_ROWS_NSA), lambda i: (i, 0, 0)),
                   pl.BlockSpec((tm, LANES), row), pl.BlockSpec((tm, LANES), row),
                   pl.BlockSpec((tm // gq, LANES, gq), lambda i: (i, 0, 0)),
                   pl.BlockSpec((1, LANES), const)],
        out_shape=[jax.ShapeDtypeStruct((m, 256), F32), jax.ShapeDtypeStruct((m, 1024), F32),
                   jax.ShapeDtypeStruct((m, 256), F32),
                   jax.ShapeDtypeStruct((KV_ROWS_ALL, m), F32),
                   jax.ShapeDtypeStruct((n, KV_ROWS_ALL, tm), BF16),
                   jax.ShapeDtypeStruct((QT_ROWS, m), BF16),
                   jax.ShapeDtypeStruct((m, KN_COLS), BF16),
                   jax.ShapeDtypeStruct((n, tm // CMP_BLOCK, KV_ROWS_NSA), F32),
                   jax.ShapeDtypeStruct((m, LANES), F32), jax.ShapeDtypeStruct((m, LANES), F32),
                   jax.ShapeDtypeStruct((m // gq, LANES, gq), F32),
                   jax.ShapeDtypeStruct((1, LANES), F32)],
        scratch_shapes=[pltpu.VMEM((1, LANES), F32), pltpu.VMEM((1, LANES), F32)],
        compiler_params=_cparams(("arbitrary",)),
        name="mixer_proj",
    )(x, g, lw["wq"], lw["wkvt"], lw["wqt"], lw["wkn"], lw["wcmp"], lw["wsm"], lw["brow"], lw["cw"])
    keys = ("sbq", "nsaq", "foxq", "kvt", "kvt3", "qt", "kn", "ck", "small", "fcum", "gt3", "kmax")
    return dict(zip(keys, outs))


def _merge_kernel(x_ref, osb_ref, onsa_ref, ofox_ref, pre_ref, post_ref, wg_ref, wsb_ref, wnsa_ref, wfox_ref,
                  wout_ref, o_ref):
    x = x_ref[...]
    h = _rms(x, pre_ref[...]).astype(BF16)
    mixed = jnp.zeros(x.shape, F32)
    for b, (o_b, w_b) in enumerate(((osb_ref, wsb_ref), (onsa_ref, wnsa_ref), (ofox_ref, wfox_ref))):
        gate = _sigmoid(jnp.dot(h, wg_ref[:, b * D_MODEL:(b + 1) * D_MODEL], preferred_element_type=F32))
        mixed = mixed + gate * jnp.dot(o_b[...].astype(BF16), w_b[...], preferred_element_type=F32)
    y = jnp.dot(mixed.astype(BF16), wout_ref[...], preferred_element_type=F32)
    o_ref[...] = x + _rms(y, post_ref[...])


def _merge(x, o_sb, o_nsa, o_fox, pre_g, post_g, lw, tm):
    m = x.shape[0]
    const = lambda i: (0, 0)
    row = lambda i: (i, 0)
    return pl.pallas_call(
        _merge_kernel,
        grid=(m // tm,),
        in_specs=[pl.BlockSpec((tm, D_MODEL), row), pl.BlockSpec((tm, 256), row), pl.BlockSpec((tm, 1024), row),
                  pl.BlockSpec((tm, 256), row), pl.BlockSpec((1, D_MODEL), const), pl.BlockSpec((1, D_MODEL), const),
                  pl.BlockSpec((D_MODEL, 3 * D_MODEL), const), pl.BlockSpec((256, D_MODEL), const),
                  pl.BlockSpec((1024, D_MODEL), const), pl.BlockSpec((256, D_MODEL), const),
                  pl.BlockSpec((D_MODEL, D_MODEL), const)],
        out_specs=pl.BlockSpec((tm, D_MODEL), row),
        out_shape=jax.ShapeDtypeStruct((m, D_MODEL), F32),
        compiler_params=_cparams(("arbitrary",)),
        name="mixer_merge",
    )(x, o_sb, o_nsa, o_fox, pre_g, post_g, lw["wgate"], lw["wbr_sb"], lw["wbr_nsa"], lw["wbr_fox"], lw["wout"])


def _sb_prompt_kernel(q_ref, kv_ref, o_ref, carry_ref, acc_ref, *, tq, tk):
    i = pl.program_id(0)
    n_rows = SB_HEADS * tq
    qbd = _stack_heads(q_ref[...], SB_HEADS)
    u = _strict_upper(tk)
    kb0 = (i * tq) // tk
    t_pos = i * tq + _row_t(n_rows, tq)
    carry_ref[...] = jnp.zeros(carry_ref.shape, F32)
    acc_ref[...] = jnp.zeros(acc_ref.shape, F32)

    def tile(kb, diagonal):
        k_t = kv_ref[kb, 0:256, :]
        v_t = kv_ref[kb, 256:512, :]
        valid = (kb * tk + _iota((1, tk), 1) < t_pos) if diagonal else None
        _sb_step(qbd, k_t, v_t, u, valid, carry_ref, acc_ref)

    tile(kb0, True)

    def cond(c):
        return jnp.logical_and(c[0] >= 0, c[1] > -SB_EXIT)

    def body(c):
        tile(c[0], False)
        return c[0] - 1, jnp.max(carry_ref[...])

    lax.while_loop(cond, body, (kb0 - 1, jnp.max(carry_ref[...])))
    o_ref[...] = _diag_select(acc_ref[...], SB_HEADS, tq).astype(o_ref.dtype)


def _sb_prompt(sbq, kvt3, tq):
    t = sbq.shape[0]
    n_kt, _, tk = kvt3.shape
    return pl.pallas_call(
        functools.partial(_sb_prompt_kernel, tq=tq, tk=tk),
        grid=(t // tq,),
        in_specs=[pl.BlockSpec((tq, 256), lambda i: (i, 0)),
                  pl.BlockSpec((n_kt, KV_ROWS_SB, tk), lambda i: (0, ROW_SB // KV_ROWS_SB, 0))],
        out_specs=pl.BlockSpec((tq, 256), lambda i: (i, 0)),
        out_shape=jax.ShapeDtypeStruct((t, 256), BF16),
        scratch_shapes=[pltpu.VMEM((SB_HEADS * tq, 1), F32), pltpu.VMEM((SB_HEADS * tq, 256), F32)],
        compiler_params=_cparams(("arbitrary",)),
        name="sb_prompt",
    )(sbq, kvt3)


def _lanes_per_head(row, n_heads, tq, lane0):
    return jnp.concatenate([jnp.broadcast_to(row[:, lane0 + h:lane0 + h + 1], (1, tq)) for h in range(n_heads)], axis=1)


def _fox_prompt_kernel(qt_ref, k_ref, vt_ref, fcum_ref, kmax_ref, o_ref, m_ref, l_ref, acc_ref, *, tq, tk):
    i = pl.program_id(0)
    n_lanes = FOX_HEADS * tq
    qt = qt_ref[...]
    row_head = _iota(qt.shape, 0) // HEAD_DIM
    qbd_t = jnp.concatenate([jnp.where(row_head == h, qt, jnp.zeros_like(qt)) for h in range(FOX_HEADS)], axis=1)
    kb0 = (i * tq) // tk
    t_pos = i * tq + _iota((1, n_lanes), 1) % tq
    q32 = qbd_t.astype(F32)
    q_norm = jnp.sqrt(jnp.sum(q32 * q32, axis=0, keepdims=True))
    z_bound = 1.02 * q_norm * jnp.sqrt(_lanes_per_head(kmax_ref[...], FOX_HEADS, tq, 0))
    _flash_init(m_ref, l_ref, acc_ref)

    v_tiles = tk // vt_ref.shape[2]

    def tile(kb, diagonal):
        start = pl.multiple_of(kb * tk, tk)
        f = fcum_ref[pl.ds(start, tk), :]
        fk = jnp.concatenate([jnp.broadcast_to(f[:, LOGF_LANE + h:LOGF_LANE + h + 1], (tk, tq))
                              for h in range(FOX_HEADS)], axis=1)
        s = jnp.dot(k_ref[pl.ds(start, tk), :], qbd_t, preferred_element_type=F32) - fk
        if diagonal:
            s = jnp.where(kb * tk + _iota((tk, n_lanes), 0) <= t_pos, s, NEG_INF)
        v_t = jnp.concatenate([vt_ref[kb * v_tiles + j] for j in range(v_tiles)], axis=1)
        _flash_update_t(s, v_t, m_ref, l_ref, acc_ref)

    def live(kb):
        f_end = fcum_ref[pl.ds(kb * tk + tk - 1, 1), :]
        bound = z_bound - _lanes_per_head(f_end, FOX_HEADS, tq, LOGF_LANE)
        return jnp.max(bound - m_ref[...]) > -EXP_ZERO

    tile(kb0, True)

    def cond(c):
        return c[1]

    def body(c):
        tile(c[0], False)
        nxt = c[0] - 1
        return nxt, jnp.logical_and(nxt >= 0, live(jnp.maximum(nxt, 0)))

    first = kb0 - 1
    lax.while_loop(cond, body, (first, jnp.logical_and(first >= 0, live(jnp.maximum(first, 0)))))
    o_t = _flash_out(l_ref, acc_ref)
    o_t = jnp.concatenate([o_t[h * HEAD_DIM:(h + 1) * HEAD_DIM, h * tq:(h + 1) * tq] for h in range(FOX_HEADS)], axis=0)
    o_ref[...] = o_t.T.astype(o_ref.dtype)


def _resident(shape, index):
    return pl.BlockSpec(shape, index, pipeline_mode=pl.Buffered(1))


def _fox_prompt(qt, kn, kvt3, fcum, kmax, tq, tk):
    t = kn.shape[0]
    n_vt, _, tv = kvt3.shape
    n_lanes = FOX_HEADS * tq
    assert tk % tv == 0 and tk % tq == 0 and t % tk == 0
    return pl.pallas_call(
        functools.partial(_fox_prompt_kernel, tq=tq, tk=tk),
        grid=(t // tq,),
        in_specs=[pl.BlockSpec((256, tq), lambda i: (4, i)),
                  _resident((t, 256), lambda i: (0, 0)),
                  _resident((n_vt, 256, tv), lambda i: (0, (ROW_FOX + 256) // 256, 0)),
                  _resident((t, LANES), lambda i: (0, 0)),
                  pl.BlockSpec((1, LANES), lambda i: (0, 0))],
        out_specs=pl.BlockSpec((tq, 256), lambda i: (i, 0)),
        out_shape=jax.ShapeDtypeStruct((t, 256), BF16),
        scratch_shapes=[pltpu.VMEM((1, n_lanes), F32), pltpu.VMEM((1, n_lanes), F32), pltpu.VMEM((256, n_lanes), F32)],
        compiler_params=_cparams(("arbitrary",)),
        name="fox_prompt",
    )(qt, kn, kvt3, fcum, kmax)


def _group_rows(x_g0, x_g1):
    return jnp.concatenate([x_g0] * NSA_GROUP_SIZE + [x_g1] * NSA_GROUP_SIZE, axis=0)


def _slope_lanes(heads, tq):
    return jnp.concatenate([jnp.full((1, tq), 2.0 ** -(h + 1), F32) for h in heads], axis=1)


def _stack_nsa_t(qt_ref, heads):
    return jnp.concatenate([qt_ref[h * LANES:(h + 1) * LANES, :] for h in heads], axis=1)


def _nsa_select_kernel(qt_ref, ck_ref, ckt_ref, ocmp_ref, sel_ref, flag_ref, *, tq, tk, n_blocks):
    i = pl.program_id(0)
    n_lanes = NSA_HEADS * tq
    qbd_t = _stack_nsa_t(qt_ref, range(NSA_HEADS))
    slope = _slope_lanes(range(NSA_HEADS), tq)
    t_pos = i * tq + _iota((1, n_lanes), 1) % tq
    cmp_end = _iota((n_blocks, n_lanes), 0) * CMP_BLOCK + (CMP_BLOCK - 1)
    s_c = jnp.dot(ck_ref[:, 0:128].astype(BF16), qbd_t, preferred_element_type=F32)
    s_c = s_c - slope * (t_pos - cmp_end).astype(F32)
    valid = cmp_end <= t_pos
    s_c = jnp.where(valid, s_c, NEG_INF)
    p = jnp.where(valid, jnp.exp(s_c - jnp.max(s_c, axis=0, keepdims=True)), 0.0)
    p_c = p / jnp.maximum(jnp.sum(p, axis=0, keepdims=True), 1e-30)
    ocmp_ref[...] = jnp.dot(ckt_ref[128:256, :].astype(BF16), p_c.astype(BF16), preferred_element_type=F32)

    cur = (i * tq + _iota((1, tq), 1)) // SEL_BLOCK
    blk = _iota((n_blocks, tq), 0)
    forced = jnp.logical_or(jnp.logical_or(blk == cur, blk == cur - 1), blk == 0)
    n_kt = flag_ref.shape[1]
    tile_of = jnp.where(_iota((n_kt, n_blocks), 1) // (tk // SEL_BLOCK) == _iota((n_kt, n_blocks), 0), 1.0, 0.0).astype(BF16)
    for g in range(NSA_KV_GROUPS):
        score = sum(p_c[:, (g * NSA_GROUP_SIZE + r) * tq:(g * NSA_GROUP_SIZE + r + 1) * tq] for r in range(NSA_GROUP_SIZE))
        score = jnp.where(forced, FORCE_SCORE, jnp.where(blk > cur, -FORCE_SCORE, score))
        sel = _top_blocks_t(score, min(SEL_TOPK, n_blocks))
        sel_ref[g * n_blocks:(g + 1) * n_blocks, :] = jnp.where(sel > 0.5, 0.0, NEG_INF)
        hits = jnp.dot(tile_of, sel.astype(BF16), preferred_element_type=F32)
        flag_ref[g] = (jnp.max(hits, axis=1, keepdims=True) > 0.5).astype(jnp.int32)


def _nsa_select(qt, ck, ckt, tq, tk):
    t = qt.shape[1]
    n_blocks = ck.shape[0]
    n_qt, n_kt = t // tq, t // tk
    return pl.pallas_call(
        functools.partial(_nsa_select_kernel, tq=tq, tk=tk, n_blocks=n_blocks),
        grid=(n_qt,),
        in_specs=[pl.BlockSpec((NSA_HEADS * LANES, tq), lambda i: (0, i)),
                  pl.BlockSpec((n_blocks, KV_ROWS_NSA), lambda i: (0, 0)),
                  pl.BlockSpec((KV_ROWS_NSA, n_blocks), lambda i: (0, 0))],
        out_specs=[pl.BlockSpec((None, LANES, NSA_HEADS * tq), lambda i: (i, 0, 0)),
                   pl.BlockSpec((None, NSA_KV_GROUPS * n_blocks, tq), lambda i: (i, 0, 0)),
                   pl.BlockSpec((None, NSA_KV_GROUPS, n_kt, 1), lambda i: (i, 0, 0, 0))],
        out_shape=[jax.ShapeDtypeStruct((n_qt, LANES, NSA_HEADS * tq), F32),
                   jax.ShapeDtypeStruct((n_qt, NSA_KV_GROUPS * n_blocks, tq), F32),
                   jax.ShapeDtypeStruct((n_qt, NSA_KV_GROUPS, n_kt, 1), jnp.int32)],
        compiler_params=_cparams(("arbitrary",)),
        name="nsa_select",
    )(qt, ck, ckt)


def _nsa_attend_kernel(flag_ref, qt_ref, ksel_ref, vsel_ref, kwin_ref, vwin_ref, sel_ref, ocmp_ref, gate_ref,
                       o_ref, m_ref, l_ref, acc_ref, *, tq, tk, n_blocks):
    i = pl.program_id(0)
    n_kt = vsel_ref.shape[0]
    n_lanes = NSA_GROUP_SIZE * tq
    kb0 = (i * tq) // tk
    blocks_per_tile = tk // SEL_BLOCK
    tiles_per_chunk = 8 // blocks_per_tile
    assert blocks_per_tile * tiles_per_chunk == 8 and n_blocks % 8 == 0
    rel = (_iota((1, n_lanes), 1) % tq - _iota((tk, n_lanes), 0)).astype(F32)
    gates = gate_ref[...]
    for g in range(NSA_KV_GROUPS):
        heads = range(g * NSA_GROUP_SIZE, (g + 1) * NSA_GROUP_SIZE)
        qbd_t = _stack_nsa_t(qt_ref, heads)
        slope = _slope_lanes(heads, tq)
        bias_rel = slope * rel

        def tile(kb, k_ref, vt_ref, selected, diagonal):
            start = pl.multiple_of(kb * tk, tk)
            off = (i * tq - kb * tk).astype(F32)
            s = jnp.dot(k_ref[pl.ds(start, tk), :], qbd_t, preferred_element_type=F32) - bias_rel - slope * off
            if selected:
                chunk = sel_ref[pl.ds(pl.multiple_of(g * n_blocks + (kb // tiles_per_chunk) * 8, 8), 8), :]
                sub = kb % tiles_per_chunk
                rows = chunk[0:blocks_per_tile]
                for c in range(1, tiles_per_chunk):
                    rows = jnp.where(sub == c, chunk[c * blocks_per_tile:(c + 1) * blocks_per_tile], rows)
                mask = jnp.concatenate([jnp.broadcast_to(rows[b:b + 1, :], (SEL_BLOCK, tq))
                                        for b in range(blocks_per_tile)], axis=0)
                s = s + jnp.concatenate([mask] * NSA_GROUP_SIZE, axis=1)
                if diagonal:
                    s = jnp.where(rel >= -off, s, NEG_INF)
            else:
                dist = rel + off
                s = jnp.where(dist >= 0.0, jnp.where(dist < float(WINDOW), s, NEG_INF), NEG_INF)
            _flash_update_t(s, vt_ref[kb], m_ref, l_ref, acc_ref)

        _flash_init(m_ref, l_ref, acc_ref)

        def sel_body(kb, c):
            @pl.when(flag_ref[(i * NSA_KV_GROUPS + g) * n_kt + kb] > 0)
            def _():
                tile(kb, ksel_ref, vsel_ref, True, False)
            return c

        lax.fori_loop(0, kb0, sel_body, 0)
        tile(kb0, ksel_ref, vsel_ref, True, True)
        o_sel = _flash_out(l_ref, acc_ref)

        _flash_init(m_ref, l_ref, acc_ref)

        def win_body(kb, c):
            tile(kb, kwin_ref, vwin_ref, False, False)
            return c

        lax.fori_loop(jnp.maximum(i * tq - (WINDOW - 1), 0) // tk, kb0 + 1, win_body, 0)
        o_win = _flash_out(l_ref, acc_ref)

        for r, h in enumerate(heads):
            lanes = slice(r * tq, (r + 1) * tq)
            o_h = (gates[GATE_LANE + h:GATE_LANE + h + 1, :] * ocmp_ref[:, h * tq:(h + 1) * tq]
                   + gates[GATE_LANE + NSA_HEADS + h:GATE_LANE + NSA_HEADS + h + 1, :] * o_sel[:, lanes]
                   + gates[GATE_LANE + 2 * NSA_HEADS + h:GATE_LANE + 2 * NSA_HEADS + h + 1, :] * o_win[:, lanes])
            o_ref[:, h * LANES:(h + 1) * LANES] = o_h.T.astype(o_ref.dtype)


def _nsa_attend(flags, qt, kn, kvt3, sel, ocmp, gt3, tq):
    t = kn.shape[0]
    n_kt, _, tk = kvt3.shape
    n_blocks = sel.shape[1] // NSA_KV_GROUPS
    n_lanes = NSA_GROUP_SIZE * tq
    grid_spec = pltpu.PrefetchScalarGridSpec(
        num_scalar_prefetch=1, grid=(t // tq,),
        in_specs=[pl.BlockSpec((NSA_HEADS * LANES, tq), lambda i, f: (0, i)),
                  _resident((t, LANES), lambda i, f: (0, 2)),
                  _resident((n_kt, LANES, tk), lambda i, f: (0, (ROW_SEL + LANES) // LANES, 0)),
                  _resident((t, LANES), lambda i, f: (0, 3)),
                  _resident((n_kt, LANES, tk), lambda i, f: (0, (ROW_WIN + LANES) // LANES, 0)),
                  pl.BlockSpec((None, NSA_KV_GROUPS * n_blocks, tq), lambda i, f: (i, 0, 0)),
                  pl.BlockSpec((None, LANES, NSA_HEADS * tq), lambda i, f: (i, 0, 0)),
                  pl.BlockSpec((None, LANES, tq), lambda i, f: (i, 0, 0))],
        out_specs=pl.BlockSpec((tq, NSA_HEADS * LANES), lambda i, f: (i, 0)),
        scratch_shapes=[pltpu.VMEM((1, n_lanes), F32), pltpu.VMEM((1, n_lanes), F32), pltpu.VMEM((LANES, n_lanes), F32)])
    return pl.pallas_call(
        functools.partial(_nsa_attend_kernel, tq=tq, tk=tk, n_blocks=n_blocks), grid_spec=grid_spec,
        out_shape=jax.ShapeDtypeStruct((t, NSA_HEADS * LANES), BF16),
        compiler_params=_cparams(("arbitrary",)), name="nsa_attend",
    )(flags, qt, kn, kvt3, kn, kvt3, sel, ocmp, gt3)


def _page_specs(rows, n_pages, layer, reverse):
    def spec(j):
        def index(b, s, pt):
            slot = s * PAGES_PER_STEP + j
            if reverse:
                slot = n_pages - 1 - slot
            return (layer, pt[b, slot], 0, 0)
        return pl.BlockSpec((None, None, rows, PAGE_SIZE), index)
    return [spec(j) for j in range(PAGES_PER_STEP)]


def _page_suffix_sums(x, u, n_split):
    r = x.shape[0]
    n = x.shape[1] // PAGE_SIZE
    stacked = jnp.concatenate([x[:, j * PAGE_SIZE:(j + 1) * PAGE_SIZE] for j in range(n)], axis=0)
    y = _dot_split_lhs(stacked, u, n_split)
    tot = y[:, 0:1] + stacked[:, 0:1]
    inner = jnp.concatenate([y[j * r:(j + 1) * r] for j in range(n)], axis=1)
    return inner, [tot[j * r:(j + 1) * r] for j in range(n)]


def _page_offsets(carry, totals):
    r = carry.shape[0]
    cols = []
    for tot in totals:
        cols.append(jnp.broadcast_to(carry, (r, PAGE_SIZE)))
        carry = carry + tot
    return jnp.concatenate(cols, axis=1), carry


def _sb_pages_step(qbd, pages, u, carry_ref, acc_ref):
    kv = jnp.concatenate([page[...] for page in pages], axis=1).astype(BF16)
    z = jnp.dot(qbd, kv[0:256], preferred_element_type=F32)
    log_keep = -_softplus(z)
    inner, totals = _page_suffix_sums(log_keep, u, 2)
    offsets, carry = _page_offsets(carry_ref[...], totals)
    w = jnp.exp(z + log_keep + inner + offsets)
    acc_ref[...] += lax.dot_general(w.astype(BF16), kv[256:512], NT, preferred_element_type=F32)
    carry_ref[...] = carry


def _sb_sample_head_kernel(pt_ref, q_ref, new_ref, *rest, n_new):
    pages = rest[:PAGES_PER_STEP]
    acc_ref, carry_ref, live_ref = rest[PAGES_PER_STEP:]
    n_rows = SB_HEADS * n_new
    qbd = _stack_heads(q_ref[...], SB_HEADS)
    u = _strict_upper(PAGE_SIZE)
    carry_ref[...] = jnp.zeros(carry_ref.shape, F32)
    acc_ref[...] = jnp.zeros(acc_ref.shape, F32)
    new = new_ref[...].astype(BF16)
    valid = _iota((1, PAGE_SIZE), 1) < _row_t(n_rows, n_new)
    _sb_step(qbd, new[0:256], new[256:512], u, valid, carry_ref, acc_ref)
    _sb_pages_step(qbd, pages, u, carry_ref, acc_ref)
    live = jnp.max(carry_ref[...], axis=0, keepdims=True) > -SB_EXIT
    live_ref[...] = jnp.broadcast_to(live, live_ref.shape).astype(jnp.int32)


def _sb_sample_tail_kernel(pt_ref, live_ref, q_ref, accin_ref, carryin_ref, *rest, n_new):
    pages = rest[:PAGES_PER_STEP]
    o_ref, carry_ref, acc_ref = rest[PAGES_PER_STEP:]
    b, s = pl.program_id(0), pl.program_id(1)

    @pl.when(s == 0)
    def _():
        carry_ref[...] = carryin_ref[...]
        acc_ref[...] = accin_ref[...]

    @pl.when(jnp.logical_and(live_ref[b] > 0, jnp.max(carry_ref[...]) > -SB_EXIT))
    def _():
        _sb_pages_step(_stack_heads(q_ref[...], SB_HEADS), pages, _strict_upper(PAGE_SIZE), carry_ref, acc_ref)

    @pl.when(s == pl.num_programs(1) - 1)
    def _():
        o_ref[...] = _diag_select(acc_ref[...], SB_HEADS, n_new)


def _sb_sample(q, new_t, cache_t, page_table, layer, n_new):
    n_seq, n_pages = page_table.shape
    n_rows = SB_HEADS * n_new
    n_steps = n_pages // PAGES_PER_STEP
    assert n_steps >= 2
    caches = [cache_t] * PAGES_PER_STEP
    head_spec = pltpu.PrefetchScalarGridSpec(
        num_scalar_prefetch=1, grid=(n_seq, 1),
        in_specs=[pl.BlockSpec((n_new, 256), lambda b, s, pt: (b, 0)),
                  pl.BlockSpec((None, KV_ROWS_SB, PAGE_SIZE), lambda b, s, pt: (b, 0, 0))]
        + _page_specs(KV_ROWS_SB, n_pages, layer, True),
        out_specs=[pl.BlockSpec((n_rows, 256), lambda b, s, pt: (b, 0)),
                   pl.BlockSpec((n_rows, 1), lambda b, s, pt: (b, 0)),
                   pl.BlockSpec((None, 8, LANES), lambda b, s, pt: (b, 0, 0))])
    acc, carry, live = pl.pallas_call(
        functools.partial(_sb_sample_head_kernel, n_new=n_new), grid_spec=head_spec,
        out_shape=[jax.ShapeDtypeStruct((n_seq * n_rows, 256), F32), jax.ShapeDtypeStruct((n_seq * n_rows, 1), F32),
                   jax.ShapeDtypeStruct((n_seq, 8, LANES), jnp.int32)],
        compiler_params=_cparams(("arbitrary", "arbitrary")), name="sb_sample_head",
    )(page_table, q, new_t, *caches)

    def tail_page(j):
        def index(b, s, pt, lv):
            slot = n_pages - 1 - ((s + 1) * PAGES_PER_STEP + j)
            return (layer, jnp.where(lv[b] > 0, pt[b, slot], pt[0, 0]), 0, 0)
        return pl.BlockSpec((None, None, KV_ROWS_SB, PAGE_SIZE), index)

    tail_spec = pltpu.PrefetchScalarGridSpec(
        num_scalar_prefetch=2, grid=(n_seq, n_steps - 1),
        in_specs=[pl.BlockSpec((n_new, 256), lambda b, s, pt, lv: (b, 0)),
                  pl.BlockSpec((n_rows, 256), lambda b, s, pt, lv: (b, 0)),
                  pl.BlockSpec((n_rows, 1), lambda b, s, pt, lv: (b, 0))]
        + [tail_page(j) for j in range(PAGES_PER_STEP)],
        out_specs=pl.BlockSpec((n_new, 256), lambda b, s, pt, lv: (b, 0)),
        scratch_shapes=[pltpu.VMEM((n_rows, 1), F32), pltpu.VMEM((n_rows, 256), F32)])
    return pl.pallas_call(
        functools.partial(_sb_sample_tail_kernel, n_new=n_new), grid_spec=tail_spec,
        out_shape=jax.ShapeDtypeStruct((n_seq * n_new, 256), F32),
        compiler_params=_cparams(("arbitrary", "arbitrary")), name="sb_sample_tail",
    )(page_table, live[:, 0, 0], q, acc, carry, *caches)


def _expand_heads(lf, n_new):
    n = lf.shape[1]
    return jnp.concatenate([jnp.broadcast_to(lf[h:h + 1, :], (n_new, n)) for h in range(FOX_HEADS)], axis=0)


def _fox_sample_kernel(pt_ref, q_ref, new_ref, lfnew_ref, *rest, n_new):
    pages = rest[:PAGES_PER_STEP]
    lf_pages = rest[PAGES_PER_STEP:2 * PAGES_PER_STEP]
    o_ref, m_ref, l_ref, acc_ref, sfx_ref, cnew_ref = rest[2 * PAGES_PER_STEP:]
    s = pl.program_id(1)
    n_rows = FOX_HEADS * n_new
    qbd = _stack_heads(q_ref[...], FOX_HEADS)
    u = _strict_upper(PAGE_SIZE)
    lane = _iota((1, PAGE_SIZE), 1)
    t_row = _row_t(n_rows, n_new)

    @pl.when(s == 0)
    def _():
        _flash_init(m_ref, l_ref, acc_ref)
        sfx_ref[...] = jnp.zeros(sfx_ref.shape, F32)
        lf = _expand_heads(lfnew_ref[...], n_new)
        incl = jnp.where(_iota((PAGE_SIZE, PAGE_SIZE), 0) <= _iota((PAGE_SIZE, PAGE_SIZE), 1), 1.0, 0.0).astype(BF16)
        cum = _dot_split_lhs(lf, incl, 3)
        c_t = jnp.sum(jnp.where(lane == t_row, cum, 0.0), axis=1, keepdims=True)
        cnew_ref[...] = c_t
        new = new_ref[...].astype(BF16)
        sc = jnp.dot(qbd, new[0:256], preferred_element_type=F32) + (c_t - cum)
        sc = jnp.where(lane <= t_row, sc, NEG_INF)
        _flash_update(sc, new[256:512], m_ref, l_ref, acc_ref)

    kv = jnp.concatenate([page[...] for page in pages], axis=1).astype(BF16)
    lf = _expand_heads(jnp.concatenate([lf_page[...] for lf_page in lf_pages], axis=1), n_new)
    inner, totals = _page_suffix_sums(lf, u, 3)
    offsets, sfx = _page_offsets(sfx_ref[...], totals)
    sc = jnp.dot(qbd, kv[0:256], preferred_element_type=F32) + (inner + offsets + cnew_ref[...])
    _flash_update(sc, kv[256:512], m_ref, l_ref, acc_ref)
    sfx_ref[...] = sfx

    @pl.when(s == pl.num_programs(1) - 1)
    def _():
        o_ref[...] = _diag_select(_flash_out(l_ref, acc_ref), FOX_HEADS, n_new)


def _fox_sample(q, new_t, lf_new, cache_t, lf_cache_t, page_table, layer, n_new):
    n_seq, n_pages = page_table.shape
    n_rows = FOX_HEADS * n_new
    grid_spec = pltpu.PrefetchScalarGridSpec(
        num_scalar_prefetch=1, grid=(n_seq, n_pages // PAGES_PER_STEP),
        in_specs=[pl.BlockSpec((n_new, 256), lambda b, s, pt: (b, 0)),
                  pl.BlockSpec((None, KV_ROWS_FOX, PAGE_SIZE), lambda b, s, pt: (b, 0, 0)),
                  pl.BlockSpec((None, FOX_HEADS, PAGE_SIZE), lambda b, s, pt: (b, 0, 0))]
        + _page_specs(KV_ROWS_FOX, n_pages, layer, True) + _page_specs(FOX_HEADS, n_pages, layer, True),
        out_specs=pl.BlockSpec((n_new, 256), lambda b, s, pt: (b, 0)),
        scratch_shapes=[pltpu.VMEM((n_rows, 1), F32), pltpu.VMEM((n_rows, 1), F32), pltpu.VMEM((n_rows, 256), F32),
                        pltpu.VMEM((n_rows, 1), F32), pltpu.VMEM((n_rows, 1), F32)])
    return pl.pallas_call(
        functools.partial(_fox_sample_kernel, n_new=n_new), grid_spec=grid_spec,
        out_shape=jax.ShapeDtypeStruct((n_seq * n_new, 256), F32),
        compiler_params=_cparams(("arbitrary", "arbitrary")), name="fox_sample",
    )(page_table, q, new_t, lf_new, *([cache_t] * PAGES_PER_STEP), *([lf_cache_t] * PAGES_PER_STEP))


def _write_head_padded(o_all, n_new, o_ref):
    for h in range(NSA_HEADS):
        o_ref[:, h * LANES:(h + 1) * LANES] = o_all[h * n_new:(h + 1) * n_new]


def _cmp_sample_kernel(pt_ref, q_ref, cw_ref, *rest, n_new, past_len):
    pages = rest[:PAGES_PER_STEP]
    o_ref, sel_ref, need_ref, ckt_ref = rest[PAGES_PER_STEP:]
    s = pl.program_id(1)
    n_rows = NSA_HEADS * n_new
    n_chunks, _, chunk_lanes = ckt_ref.shape
    n_blocks = n_chunks * chunk_lanes
    n_keys = PAGES_PER_STEP * PAGE_SIZE
    blocks_per_step = n_keys // CMP_BLOCK
    steps_per_chunk = chunk_lanes // blocks_per_step
    blk_lane = _iota((1, n_blocks), 1)

    @pl.when(s == 0)
    def _():
        ckt_ref[...] = jnp.zeros(ckt_ref.shape, F32)

    keys = jnp.concatenate([page[...] for page in pages], axis=1).astype(BF16)
    col = (s % steps_per_chunk) * blocks_per_step + _iota((n_keys, chunk_lanes), 0) // CMP_BLOCK
    placed = jnp.where(_iota((n_keys, chunk_lanes), 1) == col, cw_ref[...], 0.0).astype(BF16)
    ckt_ref[s // steps_per_chunk] += jnp.dot(keys, placed, preferred_element_type=F32)

    @pl.when(s == pl.num_programs(1) - 1)
    def _():
        qbd = _stack_nsa(q_ref[...])
        slope = _nsa_slopes(n_rows, n_new)
        t_pos = past_len + _row_t(n_rows, n_new)
        cmp_end = blk_lane * CMP_BLOCK + (CMP_BLOCK - 1)
        ck = jnp.concatenate([ckt_ref[c] for c in range(n_chunks)], axis=1)
        s_c = jnp.dot(qbd, ck[0:128].astype(BF16), preferred_element_type=F32)
        s_c = s_c - slope * (t_pos - cmp_end).astype(F32)
        p_c = _masked_softmax_rows(s_c, cmp_end <= t_pos)
        o_cmp = lax.dot_general(p_c.astype(BF16), ck[128:256].astype(BF16), NT, preferred_element_type=F32)
        _write_head_padded(o_cmp, n_new, o_ref)
        cur = (past_len + _iota((n_new, 1), 0)) // SEL_BLOCK
        blk = _iota((n_new, n_blocks), 1)
        forced = jnp.logical_or(blk == cur - 1, blk == 0)
        for g in range(NSA_KV_GROUPS):
            score = sum(p_c[(g * NSA_GROUP_SIZE + r) * n_new:(g * NSA_GROUP_SIZE + r + 1) * n_new]
                        for r in range(NSA_GROUP_SIZE))
            score = jnp.where(forced, FORCE_SCORE, jnp.where(blk > cur, -FORCE_SCORE, score))
            sel_ref[g * n_new:(g + 1) * n_new, :] = _top_blocks(score, min(SEL_TOPK - 1, n_blocks))
        n_pages = need_ref.shape[1]
        in_page = jnp.where(_iota((n_blocks, n_pages), 0) // (PAGE_SIZE // SEL_BLOCK) == _iota((n_blocks, n_pages), 1),
                            1.0, 0.0).astype(BF16)
        hits = jnp.dot(sel_ref[...].astype(BF16), in_page, preferred_element_type=F32)
        need_ref[...] = (jnp.max(hits, axis=0, keepdims=True) > 0.5).astype(jnp.int32)


def _cmp_sample(nsaq, cw_keys, cache_t, page_table, layer, n_new, past_len):
    n_seq, n_pages = page_table.shape
    n_blocks = past_len // CMP_BLOCK
    n_keys, chunk_lanes = cw_keys.shape
    assert n_blocks % chunk_lanes == 0 and chunk_lanes % (n_keys // CMP_BLOCK) == 0
    grid_spec = pltpu.PrefetchScalarGridSpec(
        num_scalar_prefetch=1, grid=(n_seq, n_pages // PAGES_PER_STEP),
        in_specs=[pl.BlockSpec((n_new, 1024), lambda b, s, pt: (b, 0)),
                  pl.BlockSpec((n_keys, chunk_lanes), lambda b, s, pt: (0, 0))]
        + _page_specs(KV_ROWS_NSA, n_pages, layer, False),
        out_specs=[pl.BlockSpec((n_new, 1024), lambda b, s, pt: (b, 0)),
                   pl.BlockSpec((None, NSA_KV_GROUPS * n_new, n_blocks), lambda b, s, pt: (b, 0, 0)),
                   pl.BlockSpec((None, 1, n_pages), lambda b, s, pt: (b, 0, 0))],
        scratch_shapes=[pltpu.VMEM((n_blocks // chunk_lanes, KV_ROWS_NSA, chunk_lanes), F32)])
    return pl.pallas_call(
        functools.partial(_cmp_sample_kernel, n_new=n_new, past_len=past_len), grid_spec=grid_spec,
        out_shape=[jax.ShapeDtypeStruct((n_seq * n_new, 1024), F32),
                   jax.ShapeDtypeStruct((n_seq, NSA_KV_GROUPS * n_new, n_blocks), F32),
                   jax.ShapeDtypeStruct((n_seq, 1, n_pages), jnp.int32)],
        compiler_params=_cparams(("arbitrary", "arbitrary")), name="cmp_sample",
    )(page_table, nsaq, cw_keys, *([cache_t] * PAGES_PER_STEP))


def _sel_sample_kernel(pt_ref, need_ref, q_ref, new_ref, sel_ref, *rest, n_new, past_len):
    pages = rest[:PAGES_PER_STEP]
    o_ref, m_ref, l_ref, acc_ref = rest[PAGES_PER_STEP:]
    b, s = pl.program_id(0), pl.program_id(1)
    n_rows = NSA_HEADS * n_new
    n_blocks = sel_ref.shape[1]
    qbd = _stack_nsa(q_ref[...])
    slope = _nsa_slopes(n_rows, n_new)
    t_row = _row_t(n_rows, n_new)
    lane = _iota((1, PAGE_SIZE), 1)
    blocks_per_page = PAGE_SIZE // SEL_BLOCK

    @pl.when(s == 0)
    def _():
        _flash_init(m_ref, l_ref, acc_ref)
        new = new_ref[...].astype(BF16)
        dist = (t_row - lane).astype(F32)
        sc = jnp.dot(qbd, new[0:128], preferred_element_type=F32) - slope * dist
        _flash_update(jnp.where(dist >= 0.0, sc, NEG_INF), new[128:256], m_ref, l_ref, acc_ref)

    needed = sum(need_ref[b, s * PAGES_PER_STEP + j] for j in range(PAGES_PER_STEP))

    @pl.when(needed > 0)
    def _():
        sel = sel_ref[...].astype(BF16)
        n_keys = PAGES_PER_STEP * PAGE_SIZE
        kv = jnp.concatenate([page[...] for page in pages], axis=1).astype(BF16)
        expand = jnp.where(_iota((n_blocks, n_keys), 0)
                           == s * (PAGES_PER_STEP * blocks_per_page) + _iota((n_blocks, n_keys), 1) // SEL_BLOCK,
                           1.0, 0.0).astype(BF16)
        ch = jnp.dot(sel, expand, preferred_element_type=F32)
        chosen = _group_rows(ch[0:n_new], ch[n_new:2 * n_new])
        dist = (past_len + t_row - (s * n_keys + _iota((1, n_keys), 1))).astype(F32)
        sc = jnp.dot(qbd, kv[0:128], preferred_element_type=F32) - slope * dist
        _flash_update(jnp.where(chosen > 0.5, sc, NEG_INF), kv[128:256], m_ref, l_ref, acc_ref)

    @pl.when(s == pl.num_programs(1) - 1)
    def _():
        _write_head_padded(_flash_out(l_ref, acc_ref), n_new, o_ref)


def _sel_sample(nsaq, new_t, sel, need, cache_t, page_table, layer, n_new, past_len):
    n_seq, n_pages = page_table.shape
    n_rows = NSA_HEADS * n_new
    n_blocks = sel.shape[2]

    def page(j):
        def index(b, s, pt, nd):
            slot = s * PAGES_PER_STEP + j
            return (layer, jnp.where(nd[b, slot] > 0, pt[b, slot], pt[0, 0]), 0, 0)
        return pl.BlockSpec((None, None, KV_ROWS_NSA, PAGE_SIZE), index)

    grid_spec = pltpu.PrefetchScalarGridSpec(
        num_scalar_prefetch=2, grid=(n_seq, n_pages // PAGES_PER_STEP),
        in_specs=[pl.BlockSpec((n_new, 1024), lambda b, s, pt, nd: (b, 0)),
                  pl.BlockSpec((None, KV_ROWS_NSA, PAGE_SIZE), lambda b, s, pt, nd: (b, 0, 0)),
                  pl.BlockSpec((None, NSA_KV_GROUPS * n_new, n_blocks), lambda b, s, pt, nd: (b, 0, 0))]
        + [page(j) for j in range(PAGES_PER_STEP)],
        out_specs=pl.BlockSpec((n_new, 1024), lambda b, s, pt, nd: (b, 0)),
        scratch_shapes=[pltpu.VMEM((n_rows, 1), F32), pltpu.VMEM((n_rows, 1), F32), pltpu.VMEM((n_rows, LANES), F32)])
    return pl.pallas_call(
        functools.partial(_sel_sample_kernel, n_new=n_new, past_len=past_len), grid_spec=grid_spec,
        out_shape=jax.ShapeDtypeStruct((n_seq * n_new, 1024), F32),
        compiler_params=_cparams(("arbitrary", "arbitrary")), name="sel_sample",
    )(page_table, need, nsaq, new_t, sel, *([cache_t] * PAGES_PER_STEP))


def _win_sample_kernel(q_ref, new_ref, win_ref, ocmp_ref, osel_ref, gate_ref, o_ref, *, n_new):
    n_rows = NSA_HEADS * n_new
    wb = win_ref.shape[1]
    qbd = _stack_nsa(q_ref[...])
    slope = _nsa_slopes(n_rows, n_new)
    t_row = _row_t(n_rows, n_new)
    kv = jnp.concatenate([win_ref[...], new_ref[...]], axis=1).astype(BF16)
    dist = (wb + t_row - _iota((1, wb + PAGE_SIZE), 1)).astype(F32)
    sc = jnp.dot(qbd, kv[0:128], preferred_element_type=F32) - slope * dist
    valid = jnp.logical_and(dist >= 0.0, dist < float(WINDOW))
    p = _masked_softmax_rows(sc, valid)
    o_win = lax.dot_general(p.astype(BF16), kv[128:256], NT, preferred_element_type=F32)
    gates = gate_ref[...]
    for h in range(NSA_HEADS):
        cols = slice(h * LANES, (h + 1) * LANES)
        o_ref[:, cols] = (gates[:, GATE_LANE + h:GATE_LANE + h + 1] * ocmp_ref[:, cols]
                          + gates[:, GATE_LANE + NSA_HEADS + h:GATE_LANE + NSA_HEADS + h + 1] * osel_ref[:, cols]
                          + gates[:, GATE_LANE + 2 * NSA_HEADS + h:GATE_LANE + 2 * NSA_HEADS + h + 1]
                          * o_win[h * n_new:(h + 1) * n_new])


def _win_sample(nsaq, new_t, win_t, o_cmp, o_sel, small, n_new):
    n_seq, _, wb = win_t.shape
    row = lambda b: (b, 0)
    return pl.pallas_call(
        functools.partial(_win_sample_kernel, n_new=n_new),
        grid=(n_seq,),
        in_specs=[pl.BlockSpec((n_new, 1024), row),
                  pl.BlockSpec((None, KV_ROWS_NSA, PAGE_SIZE), lambda b: (b, 0, 0)),
                  pl.BlockSpec((None, KV_ROWS_NSA, wb), lambda b: (b, 0, 0)),
                  pl.BlockSpec((n_new, 1024), row), pl.BlockSpec((n_new, 1024), row),
                  pl.BlockSpec((n_new, LANES), row)],
        out_specs=pl.BlockSpec((n_new, 1024), row),
        out_shape=jax.ShapeDtypeStruct((n_seq * n_new, 1024), F32),
        compiler_params=_cparams(("arbitrary",)), name="win_sample",
    )(nsaq, new_t, win_t, o_cmp, o_sel, small)


def _layer_weights(l, w_in, b_forget, cmp_w, w_br_sb, w_br_nsa, w_br_fox, w_out, tm):
    w = w_in[l]
    scale = HEAD_DIM ** -0.5
    zeros64 = jnp.zeros((D_MODEL, HEAD_DIM), F32)
    nsa_q = w[:, 768:1280] * scale
    pads = []
    for h in range(NSA_HEADS):
        qh = nsa_q[:, h * HEAD_DIM:(h + 1) * HEAD_DIM]
        pads += [qh, zeros64] if h < NSA_GROUP_SIZE else [zeros64, qh]
    wq = jnp.concatenate([w[:, 0:256] * scale] + pads + [w[:, 2072:2328] * scale], axis=1)
    wkv = jnp.concatenate([w[:, 256:768], w[:, 2328:2840], w[:, 1280:1536], w[:, 1536:1792], w[:, 1792:2048]], axis=1)
    wsm = jnp.concatenate([w[:, 2048:2072], w[:, 2840:2844], jnp.zeros((D_MODEL, LANES - 28), F32)], axis=1)
    brow = jnp.zeros((1, LANES), F32).at[0, LOGF_LANE:LOGF_LANE + FOX_HEADS].set(b_forget[l])
    wn = w_br_nsa[l].reshape(NSA_HEADS, HEAD_DIM, D_MODEL)
    zrow = jnp.zeros((HEAD_DIM, D_MODEL), F32)
    rows = []
    for h in range(NSA_HEADS):
        rows += [wn[h], zrow] if h < NSA_GROUP_SIZE else [zrow, wn[h]]
    wkn = jnp.concatenate([w[:, 2328:2584], w[:, 1536:1664], w[:, 1792:1920]], axis=1)
    return dict(
        wq=wq.astype(BF16), wkvt=wkv.T.astype(BF16), wqt=wq[:, 256:1536].T.astype(BF16), wkn=wkn.astype(BF16),
        wcmp=w[:, 1280:1536].astype(BF16), wsm=wsm.astype(BF16),
        brow=brow, cw=jnp.broadcast_to(jnp.tile(cmp_w[l], tm // CMP_BLOCK)[:, None], (tm, KV_ROWS_NSA)),
        wgate=w[:, 2844:5916].astype(BF16), wbr_sb=w_br_sb[l].astype(BF16),
        wbr_nsa=jnp.concatenate(rows, axis=0).astype(BF16), wbr_fox=w_br_fox[l].astype(BF16),
        wout=w_out[l].astype(BF16))


def _cache_t(cache):
    d, n_pool, page = cache.shape[:3]
    return jnp.transpose(cache, (0, 1, 3, 4, 5, 2)).reshape(d, n_pool, -1, page)


def _new_pages(kvt_rows, n_seq, n_new):
    r = kvt_rows.shape[0]
    x = jnp.transpose(kvt_rows.reshape(r, n_seq, n_new), (1, 0, 2))
    return jnp.pad(x, ((0, 0), (0, 0), (0, PAGE_SIZE - n_new)))


def _kv_out(kvt_rows, lead_shape, heads):
    n = kvt_rows.shape[1]
    return jnp.transpose(kvt_rows.reshape(2, heads, HEAD_DIM, n), (3, 0, 1, 2)).reshape(*lead_shape, 2, heads, HEAD_DIM)


def kernel(x_prompt, x_sample, cache_sb_kv, cache_cmp_kv, cache_sel_kv, cache_fox_kv, cache_fox_logf, state_win_kv, page_table, ffn1_pre_g, ffn1_post_g, ffn1_w_gu, ffn1_w_down, mix_pre_g, mix_post_g, w_in, b_forget, cmp_w, w_br_sb, w_br_nsa, w_br_fox, w_out, ffn2_pre_g, ffn2_post_g, ffn2_w_gu, ffn2_w_down):
    depth = w_in.shape[0]
    batch, seq, _ = x_prompt.shape
    n_seq, n_new, _ = x_sample.shape
    past_len = page_table.shape[1] * PAGE_SIZE
    assert batch == 1 and seq % ATT_TILE == 0 and seq >= WINDOW + ATT_TILE
    assert past_len % SEL_BLOCK == 0 and n_new <= CMP_BLOCK and page_table.shape[1] % PAGES_PER_STEP == 0
    m_s = n_seq * n_new
    tm_s = m_s if m_s <= ATT_TILE else ATT_TILE
    assert m_s % tm_s == 0 and tm_s % CMP_BLOCK == 0

    x_p = x_prompt.reshape(seq, D_MODEL)
    x_s = x_sample.reshape(m_s, D_MODEL)
    sb_c, cmp_c, sel_c, fox_c = (_cache_t(c) for c in (cache_sb_kv, cache_cmp_kv, cache_sel_kv, cache_fox_kv))
    lf_c = jnp.transpose(cache_fox_logf, (0, 1, 3, 2))
    wb = state_win_kv.shape[2]
    win_c = jnp.transpose(state_win_kv, (0, 1, 3, 4, 5, 2)).reshape(depth, n_seq, KV_ROWS_NSA, wb)
    row = lambda a, l: a[l][None, :]
    outs = {k: [] for k in ("sb_p", "sb_s", "cmp_p", "cmp_s", "sel_p", "sel_s", "fox_p", "fox_s", "lf_p", "lf_s",
                            "win_p", "win_s")}
    ffn_tm_p = 512 if seq % 512 == 0 else ATT_TILE
    for l in range(depth):
        lw = _layer_weights(l, w_in, b_forget, cmp_w, w_br_sb, w_br_nsa, w_br_fox, w_out, ATT_TILE)
        lw_s = dict(lw, cw=lw["cw"][:tm_s])
        wgu1, wd1 = ffn1_w_gu[l].astype(BF16), ffn1_w_down[l].astype(BF16)
        wgu2, wd2 = ffn2_w_gu[l].astype(BF16), ffn2_w_down[l].astype(BF16)
        x_p = _ffn(x_p, row(ffn1_pre_g, l), row(ffn1_post_g, l), wgu1, wd1, ffn_tm_p)
        x_s = _ffn(x_s, row(ffn1_pre_g, l), row(ffn1_post_g, l), wgu1, wd1, tm_s)

        pp = _proj(x_p, row(mix_pre_g, l), lw, ATT_TILE)
        ck = pp["ck"].reshape(seq // CMP_BLOCK, KV_ROWS_NSA)
        o_sb = _sb_prompt(pp["sbq"], pp["kvt3"], Q_TILE)
        o_fox = _fox_prompt(pp["qt"], pp["kn"], pp["kvt3"], pp["fcum"], pp["kmax"], Q_TILE, FOX_KEYS)
        o_cmp, sel_blocks, flags = _nsa_select(pp["qt"], ck, jnp.transpose(ck), Q_TILE, ATT_TILE)
        o_nsa = _nsa_attend(flags.reshape(-1), pp["qt"], pp["kn"], pp["kvt3"], sel_blocks, o_cmp, pp["gt3"], Q_TILE)
        x_p = _merge(x_p, o_sb, o_nsa, o_fox, row(mix_pre_g, l), row(mix_post_g, l), lw, ATT_TILE)

        ps = _proj(x_s, row(mix_pre_g, l), lw_s, tm_s)
        kvt_s = ps["kvt"]
        new = lambda r0, n: _new_pages(kvt_s[r0:r0 + n], n_seq, n_new)
        lf_new = _new_pages(jnp.transpose(ps["small"][:, LOGF_LANE:LOGF_LANE + FOX_HEADS]), n_seq, n_new)
        o_sb = _sb_sample(ps["sbq"], new(ROW_SB, KV_ROWS_SB), sb_c, page_table, l, n_new)
        o_fox = _fox_sample(ps["foxq"], new(ROW_FOX, KV_ROWS_FOX), lf_new, fox_c, lf_c, page_table, l, n_new)
        step_keys = PAGES_PER_STEP * PAGE_SIZE
        cw_keys = jnp.broadcast_to(jnp.tile(cmp_w[l], step_keys // CMP_BLOCK)[:, None],
                                   (step_keys, min(LANES, past_len // CMP_BLOCK)))
        o_cmp, sel, need = _cmp_sample(ps["nsaq"], cw_keys, cmp_c, page_table, l, n_new, past_len)
        o_sel = _sel_sample(ps["nsaq"], new(ROW_SEL, KV_ROWS_NSA), sel, need[:, 0, :], sel_c, page_table, l, n_new,
                            past_len)
        new_win = new(ROW_WIN, KV_ROWS_NSA)
        o_nsa = _win_sample(ps["nsaq"], new_win, win_c[l], o_cmp, o_sel, ps["small"], n_new)
        x_s = _merge(x_s, o_sb, o_nsa, o_fox, row(mix_pre_g, l), row(mix_post_g, l), lw, tm_s)

        x_p = _ffn(x_p, row(ffn2_pre_g, l), row(ffn2_post_g, l), wgu2, wd2, ffn_tm_p)
        x_s = _ffn(x_s, row(ffn2_pre_g, l), row(ffn2_post_g, l), wgu2, wd2, tm_s)

        kvt_p = pp["kvt"]
        outs["sb_p"].append(_kv_out(kvt_p[ROW_SB:ROW_SB + 512], (batch, seq), SB_HEADS))
        outs["sb_s"].append(_kv_out(kvt_s[ROW_SB:ROW_SB + 512], (n_seq, n_new), SB_HEADS))
        outs["cmp_p"].append(_kv_out(kvt_p[ROW_CMP:ROW_CMP + 256], (batch, seq), NSA_KV_GROUPS))
        outs["cmp_s"].append(_kv_out(kvt_s[ROW_CMP:ROW_CMP + 256], (n_seq, n_new), NSA_KV_GROUPS))
        outs["sel_p"].append(_kv_out(kvt_p[ROW_SEL:ROW_SEL + 256], (batch, seq), NSA_KV_GROUPS))
        outs["sel_s"].append(_kv_out(kvt_s[ROW_SEL:ROW_SEL + 256], (n_seq, n_new), NSA_KV_GROUPS))
        outs["fox_p"].append(_kv_out(kvt_p[ROW_FOX:ROW_FOX + 512], (batch, seq), FOX_HEADS))
        outs["fox_s"].append(_kv_out(kvt_s[ROW_FOX:ROW_FOX + 512], (n_seq, n_new), FOX_HEADS))
        outs["lf_p"].append(pp["small"][:, LOGF_LANE:LOGF_LANE + FOX_HEADS].reshape(batch, seq, FOX_HEADS))
        outs["lf_s"].append(ps["small"][:, LOGF_LANE:LOGF_LANE + FOX_HEADS].reshape(n_seq, n_new, FOX_HEADS))
        w_keep = min(WINDOW, seq)
        outs["win_p"].append(_kv_out(kvt_p[ROW_WIN:ROW_WIN + 256, seq - w_keep:], (batch, w_keep), NSA_KV_GROUPS))
        win_all = jnp.concatenate([win_c[l], new_win[:, :, :n_new]], axis=2)[:, :, n_new:]
        outs["win_s"].append(jnp.transpose(win_all.reshape(n_seq, 2, NSA_KV_GROUPS, HEAD_DIM, wb), (0, 4, 1, 2, 3)))

    st = lambda k: jnp.stack(outs[k])
    return (x_p.reshape(batch, seq, D_MODEL), x_s.reshape(n_seq, n_new, D_MODEL),
            st("sb_p"), st("sb_s"), st("cmp_p"), st("cmp_s"), st("sel_p"), st("sel_s"),
            st("fox_p"), st("fox_s"), st("lf_p"), st("lf_s"), st("win_p"), st("win_s"))
```
